```python
import jax, jax.numpy as jnp
from jax import lax
import numpy as np


D_MODEL = 4096
BATCH = 32
SEQ = 256
DEPTH = 4
DEC_BATCH = 2
DEC_SEQ = 1024
PAST_LEN = 512

GRID_W = 64
ROPE_THETA = 10000.0
Q_BLOCK = 128
EPS = 1e-6
NEG_INF = -1e30

N_BRANCH = 4
BRANCH_DIM = 1024

MLA_HEADS = 8
MLA_Q_LORA = 1024
MLA_KV_LORA = 512
MLA_NOPE = 128
MLA_ROPE = 64
MLA_V = 128

SWA_HEADS = 16
SWA_KV_HEADS = 4
SWA_HEAD_DIM = 64
SWA_WINDOW = 128
SWA_BLOCK = 128

CONV_DIM = 1024
CONV_WIDTH = 3

AX_HEADS = 8
AX_KV_HEADS = 4
AX_HEAD_DIM = 128

N_GROUPS = 8
EXPERTS_PER_GROUP = 8
N_EXPERTS = N_GROUPS * EXPERTS_PER_GROUP
TOP_K = 2
D_EXPERT = 512
MOE_BLOCK = 128

DEEPNORM_ALPHA = (2 * DEPTH) ** 0.25
DEEPNORM_BETA = (8 * DEPTH) ** -0.25

MLA_SCALE = (MLA_NOPE + MLA_ROPE) ** -0.5
SWA_SCALE = SWA_HEAD_DIM ** -0.5
AX_SCALE = AX_HEAD_DIM ** -0.5

IN_SIZES = (MLA_Q_LORA, MLA_KV_LORA + MLA_ROPE,
            SWA_HEADS * SWA_HEAD_DIM, SWA_KV_HEADS * SWA_HEAD_DIM, SWA_KV_HEADS * SWA_HEAD_DIM,
            CONV_DIM, CONV_DIM, CONV_DIM,
            AX_HEADS * AX_HEAD_DIM, AX_KV_HEADS * AX_HEAD_DIM, AX_KV_HEADS * AX_HEAD_DIM,
            N_BRANCH * D_MODEL)
IN_DIM = sum(IN_SIZES)
IN_OFFSETS = tuple(int(v) for v in np.cumsum(IN_SIZES)[:-1])

kernel_name = 'hybrid_diffusion_prefix_trunk_step'


def layer_norm(x, g, b):
    xf = x.astype(jnp.float32)
    mu = jnp.mean(xf, -1, keepdims=True)
    var = jnp.mean(jnp.square(xf - mu), -1, keepdims=True)
    return ((xf - mu) * lax.rsqrt(var + EPS) * g.astype(jnp.float32) + b.astype(jnp.float32)).astype(x.dtype)


def rms_norm(x, g):
    xf = x.astype(jnp.float32)
    return (xf * lax.rsqrt(jnp.mean(jnp.square(xf), -1, keepdims=True) + EPS) * g.astype(jnp.float32)).astype(x.dtype)


def grid_positions(t):
    rows = t // GRID_W
    row = jnp.repeat(jnp.arange(rows, dtype=jnp.int32), GRID_W)
    col = jnp.tile(jnp.arange(GRID_W, dtype=jnp.int32), rows)
    return row, col


def rope_1d(x, pos):
    m = x.shape[-1]
    inv = ROPE_THETA ** (-jnp.arange(0, m, 2, dtype=jnp.float32) / m)
    ang = pos.astype(jnp.float32)[:, None] * inv[None, :]
    cos, sin = jnp.cos(ang)[:, None, :], jnp.sin(ang)[:, None, :]
    x1 = x[..., : m // 2].astype(jnp.float32)
    x2 = x[..., m // 2:].astype(jnp.float32)
    return jnp.concatenate([x1 * cos - x2 * sin, x1 * sin + x2 * cos], -1).astype(x.dtype)


def axial_rope(x, row, col):
    h = x.shape[-1] // 2
    return jnp.concatenate([rope_1d(x[..., :h], row), rope_1d(x[..., h:], col)], -1)


def attend(q, k, v, scale, sink=None):
    b, t, h, dq = q.shape
    g = k.shape[2]
    r = h // g
    nb = t // Q_BLOCK
    qb = q.reshape(b, nb, Q_BLOCK, g, r, dq).transpose(1, 0, 2, 3, 4, 5)

    def block(qi):
        s = jnp.einsum('bqgrd,bsgd->bgrqs', qi, k, preferred_element_type=jnp.float32) * scale
        if sink is not None:
            sk = jnp.broadcast_to(sink.astype(jnp.float32).reshape(1, g, r, 1, 1), s.shape[:-1] + (1,))
            p = jax.nn.softmax(jnp.concatenate([s, sk], -1), -1)[..., :-1]
        else:
            p = jax.nn.softmax(s, -1)
        return jnp.einsum('bgrqs,bsge->bqgre', p.astype(v.dtype), v)

    o = lax.map(block, qb)
    return o.transpose(1, 0, 2, 3, 4, 5).reshape(b, t, h, v.shape[-1])


def band_attend(q, k, v, ck, cv, sink, scale):
    b, t, h, d = q.shape
    g = k.shape[2]
    r = h // g
    w = SWA_BLOCK
    nb = t // w
    qb = q.reshape(b, nb, w, g, r, d)

    def band(a):
        ap = jnp.pad(a, ((0, 0), (w, w), (0, 0), (0, 0))).reshape(b, nb + 2, w, g, a.shape[-1])
        return jnp.concatenate([ap[:, :-2], ap[:, 1:-1], ap[:, 2:]], axis=2)

    kb, vb = band(k), band(v)
    qpos = jnp.arange(nb)[:, None, None] * w + jnp.arange(w)[None, :, None]
    kpos = jnp.arange(nb)[:, None, None] * w - w + jnp.arange(3 * w)[None, None, :]
    ok = (jnp.abs(kpos - qpos) <= SWA_WINDOW) & (kpos >= 0) & (kpos < t)
    s_loc = jnp.einsum('bnqgrd,bnkgd->bngrqk', qb, kb, preferred_element_type=jnp.float32) * scale
    s_loc = jnp.where(ok[None, :, None, None], s_loc, NEG_INF)
    s_ctx = jnp.einsum('bnqgrd,bsgd->bngrqs', qb, ck, preferred_element_type=jnp.float32) * scale
    sk = jnp.broadcast_to(sink.astype(jnp.float32).reshape(1, 1, g, r, 1, 1), s_loc.shape[:-1] + (1,))
    p = jax.nn.softmax(jnp.concatenate([s_ctx, s_loc, sk], -1), -1)
    n_ctx = ck.shape[1]
    p_ctx = p[..., :n_ctx].astype(v.dtype)
    p_loc = p[..., n_ctx:n_ctx + 3 * w].astype(v.dtype)
    o = jnp.einsum('bngrqs,bsge->bnqgre', p_ctx, cv) + jnp.einsum('bngrqk,bnkge->bnqgre', p_loc, vb)
    return o.reshape(b, t, h, d)


def short_conv(u, w, bias):
    out = lax.conv_general_dilated(u, w[:, None, :], window_strides=(1,),
                                   padding=((CONV_WIDTH // 2, CONV_WIDTH // 2),),
                                   dimension_numbers=('NWC', 'WIO', 'NWC'),
                                   feature_group_count=u.shape[-1])
    return out + bias


def mla_keys(ckv, kpe, w_kv_up):
    b, s, _ = ckv.shape
    kv = (ckv @ w_kv_up).reshape(b, s, MLA_HEADS, MLA_NOPE + MLA_V)
    k = jnp.concatenate([kv[..., :MLA_NOPE],
                         jnp.broadcast_to(kpe[:, :, None, :], (b, s, MLA_HEADS, MLA_ROPE))], -1)
    return k, kv[..., MLA_NOPE:]


def token_mixers(h, p, ctx, pos):
    b, t, _ = h.shape
    (q_lat, kv_lat, s_q, s_k, s_v, c_b, c_c, c_x, a_q, a_k, a_v, gates) = jnp.split(h @ p['w_in'], IN_OFFSETS, axis=-1)
    q = (rms_norm(q_lat, p['mla_q_norm']) @ p['mla_w_q_up']).reshape(b, t, MLA_HEADS, MLA_NOPE + MLA_ROPE)
    q_nope, q_pe = q[..., :MLA_NOPE], q[..., MLA_NOPE:]
    ckv = rms_norm(kv_lat[..., :MLA_KV_LORA], p['mla_kv_norm'])
    kpe = kv_lat[..., MLA_KV_LORA:]
    s_q = s_q.reshape(b, t, SWA_HEADS, SWA_HEAD_DIM)
    s_k = s_k.reshape(b, t, SWA_KV_HEADS, SWA_HEAD_DIM)
    s_v = s_v.reshape(b, t, SWA_KV_HEADS, SWA_HEAD_DIM)
    a_q = rms_norm(a_q.reshape(b, t, AX_HEADS, AX_HEAD_DIM), p['ax_q_norm'])
    a_k = rms_norm(a_k.reshape(b, t, AX_KV_HEADS, AX_HEAD_DIM), p['ax_k_norm'])
    a_v = a_v.reshape(b, t, AX_KV_HEADS, AX_HEAD_DIM)
    if pos is not None:
        row, col = pos
        q_pe = axial_rope(q_pe, row, col)
        kpe = axial_rope(kpe[:, :, None, :], row, col)[:, :, 0]
        s_q = axial_rope(s_q, row, col)
        s_k = axial_rope(s_k, row, col)
        a_q = axial_rope(a_q, row, col)
        a_k = axial_rope(a_k, row, col)
    q_a = jnp.concatenate([q_nope, q_pe], -1)
    if ctx is None:
        new_ctx = (ckv, kpe, s_k, s_v, a_k, a_v)
        k_a, v_a = mla_keys(ckv, kpe, p['mla_w_kv_up'])
        o_a = attend(q_a, k_a, v_a, MLA_SCALE)
        o_b = attend(s_q, s_k, s_v, SWA_SCALE, sink=p['swa_sink'])
        o_d = attend(a_q, a_k, a_v, AX_SCALE)
    else:
        new_ctx = None
        c_ckv, c_kpe, c_sk, c_sv, c_ak, c_av = ctx
        k_a, v_a = mla_keys(jnp.concatenate([c_ckv, ckv], 1), jnp.concatenate([c_kpe, kpe], 1), p['mla_w_kv_up'])
        o_a = attend(q_a, k_a, v_a, MLA_SCALE)
        o_b = band_attend(s_q, s_k, s_v, c_sk, c_sv, p['swa_sink'], SWA_SCALE)
        o_d = attend(a_q, jnp.concatenate([c_ak, a_k], 1), jnp.concatenate([c_av, a_v], 1), AX_SCALE)
    o_c = c_b * short_conv(c_c * c_x, p['conv_w'], p['conv_b'])
    branches = (o_a.reshape(b, t, BRANCH_DIM), o_b.reshape(b, t, BRANCH_DIM), o_c, o_d.reshape(b, t, BRANCH_DIM))
    gates = jax.nn.sigmoid(gates.reshape(b, t, N_BRANCH, D_MODEL))
    merged = gates[:, :, 0] * (branches[0] @ p['w_branch'][0])
    for i in range(1, N_BRANCH):
        merged = merged + gates[:, :, i] * (branches[i] @ p['w_branch'][i])
    return merged @ p['w_out'], new_ctx


def hier_moe(h, w_group, b_group, w_router, b_router, w1, w3, w2):
    n_tok, d = h.shape
    g_logits = jnp.matmul(h, w_group, preferred_element_type=jnp.float32) + b_group.astype(jnp.float32)
    _, g_sel = lax.top_k(g_logits, 1)
    g_w = jnp.take_along_axis(jax.nn.softmax(g_logits, -1), g_sel, -1)
    e_logits = (jnp.matmul(h, w_router, preferred_element_type=jnp.float32)
                + b_router.astype(jnp.float32)).reshape(n_tok, N_GROUPS, EXPERTS_PER_GROUP)
    e_logits = jnp.take_along_axis(e_logits, g_sel[:, :, None], axis=1)[:, 0]
    top_v, top_i = lax.top_k(e_logits, TOP_K)
    gate = g_w * jax.nn.softmax(top_v, -1)
    expert = g_sel * EXPERTS_PER_GROUP + top_i
    m = n_tok * TOP_K
    e_flat = expert.reshape(m)
    tok = jnp.repeat(jnp.arange(n_tok, dtype=jnp.int32), TOP_K)
    w_flat = gate.reshape(m)
    order = jnp.argsort(e_flat)
    e_s, tok_s, w_s = e_flat[order], tok[order], w_flat[order]
    counts = jnp.zeros((N_EXPERTS,), jnp.int32).at[e_flat].add(1)
    padded = (counts + MOE_BLOCK - 1) // MOE_BLOCK * MOE_BLOCK
    start = jnp.cumsum(counts) - counts
    pend = jnp.cumsum(padded)
    pstart = pend - padded
    dest = pstart[e_s] + jnp.arange(m, dtype=jnp.int32) - start[e_s]
    n_blocks = -(-m // MOE_BLOCK) + N_EXPERTS
    rows = jnp.zeros((n_blocks * MOE_BLOCK, d), h.dtype).at[dest].set(h[tok_s])
    block_e = jnp.clip(jnp.searchsorted(pend, jnp.arange(n_blocks) * MOE_BLOCK, side='right'), 0, N_EXPERTS - 1)

    def expert_block(args):
        xb, e = args
        return (jax.nn.silu(xb @ w1[e]) * (xb @ w3[e])) @ w2[e]

    y_rows = lax.map(expert_block, (rows.reshape(n_blocks, MOE_BLOCK, d), block_e)).reshape(-1, d)
    return jax.ops.segment_sum(y_rows[dest] * w_s[:, None].astype(h.dtype), tok_s, num_segments=n_tok)


def trunk_layer(x, mods, p, ctx, pos):
    sh1, sc1, g1, sh2, sc2, g2 = jnp.split(mods, 6, -1)
    mix, new_ctx = token_mixers(x * (1 + sc1) + sh1, p, ctx, pos)
    x = layer_norm(DEEPNORM_ALPHA * x + g1 * mix, p['ln1_g'], p['ln1_b'])
    b, t, d = x.shape
    h = (x * (1 + sc2) + sh2).reshape(b * t, d)
    ff = hier_moe(h, p['moe_w_group'], p['moe_b_group'], p['moe_w_router'], p['moe_b_router'],
                  p['moe_w1'], p['moe_w3'], p['moe_w2']).reshape(b, t, d)
    x = layer_norm(DEEPNORM_ALPHA * x + g2 * ff, p['ln2_g'], p['ln2_b'])
    return x, new_ctx


def _nrm(k, shape, scale):
    return jax.random.normal(k, shape, jnp.float32) * scale


def setup_inputs(seed: int = 0) -> dict:
    key = jax.random.key(seed)
    ks = iter(jax.random.split(key, 40))
    dsc = D_MODEL ** -0.5
    return {
        'x_prompt': _nrm(next(ks), (BATCH, SEQ, D_MODEL), 1.0),
        'x_sample': _nrm(next(ks), (DEC_BATCH, DEC_SEQ, D_MODEL), 1.0),
        'cache_mla_ckv': _nrm(next(ks), (DEC_BATCH, DEPTH, PAST_LEN, MLA_KV_LORA), 1.0),
        'cache_mla_kpe': _nrm(next(ks), (DEC_BATCH, DEPTH, PAST_LEN, MLA_ROPE), 1.0),
        'cache_swa_k': _nrm(next(ks), (DEC_BATCH, DEPTH, PAST_LEN, SWA_KV_HEADS, SWA_HEAD_DIM), 1.0),
        'cache_swa_v': _nrm(next(ks), (DEC_BATCH, DEPTH, PAST_LEN, SWA_KV_HEADS, SWA_HEAD_DIM), 1.0),
        'cache_ax_k': _nrm(next(ks), (DEC_BATCH, DEPTH, PAST_LEN, AX_KV_HEADS, AX_HEAD_DIM), 1.0),
        'cache_ax_v': _nrm(next(ks), (DEC_BATCH, DEPTH, PAST_LEN, AX_KV_HEADS, AX_HEAD_DIM), 1.0),
        'c': _nrm(next(ks), (DEC_BATCH, D_MODEL), 1.0),
        'c_ctx': _nrm(next(ks), (D_MODEL,), 1.0),
        'w_in': _nrm(next(ks), (DEPTH, D_MODEL, IN_DIM), dsc),
        'mla_q_norm': 1.0 + _nrm(next(ks), (DEPTH, MLA_Q_LORA), 0.02),
        'mla_kv_norm': 1.0 + _nrm(next(ks), (DEPTH, MLA_KV_LORA), 0.02),
        'mla_w_q_up': _nrm(next(ks), (DEPTH, MLA_Q_LORA, MLA_HEADS * (MLA_NOPE + MLA_ROPE)), MLA_Q_LORA ** -0.5),
        'mla_w_kv_up': _nrm(next(ks), (DEPTH, MLA_KV_LORA, MLA_HEADS * (MLA_NOPE + MLA_V)), MLA_KV_LORA ** -0.5),
        'swa_sink': _nrm(next(ks), (DEPTH, SWA_HEADS), 0.5),
        'conv_w': _nrm(next(ks), (DEPTH, CONV_WIDTH, CONV_DIM), CONV_WIDTH ** -0.5),
        'conv_b': _nrm(next(ks), (DEPTH, CONV_DIM), 0.02),
        'ax_q_norm': 1.0 + _nrm(next(ks), (DEPTH, AX_HEAD_DIM), 0.02),
        'ax_k_norm': 1.0 + _nrm(next(ks), (DEPTH, AX_HEAD_DIM), 0.02),
        'w_branch': _nrm(next(ks), (DEPTH, N_BRANCH, BRANCH_DIM, D_MODEL), BRANCH_DIM ** -0.5 * DEEPNORM_BETA),
        'w_out': _nrm(next(ks), (DEPTH, D_MODEL, D_MODEL), dsc * DEEPNORM_BETA),
        'w_mod': _nrm(next(ks), (DEPTH, D_MODEL, 6 * D_MODEL), 0.5 * dsc),
        'b_mod': _nrm(next(ks), (DEPTH, 6 * D_MODEL), 0.02),
        'ln1_g': 1.0 + _nrm(next(ks), (DEPTH, D_MODEL), 0.02),
        'ln1_b': _nrm(next(ks), (DEPTH, D_MODEL), 0.02),
        'ln2_g': 1.0 + _nrm(next(ks), (DEPTH, D_MODEL), 0.02),
        'ln2_b': _nrm(next(ks), (DEPTH, D_MODEL), 0.02),
        'moe_w_group': _nrm(next(ks), (DEPTH, D_MODEL, N_GROUPS), dsc),
        'moe_b_group': _nrm(next(ks), (DEPTH, N_GROUPS), 0.01),
        'moe_w_router': _nrm(next(ks), (DEPTH, D_MODEL, N_EXPERTS), dsc),
        'moe_b_router': _nrm(next(ks), (DEPTH, N_EXPERTS), 0.01),
        'moe_w1': _nrm(next(ks), (DEPTH, N_EXPERTS, D_MODEL, D_EXPERT), dsc),
        'moe_w3': _nrm(next(ks), (DEPTH, N_EXPERTS, D_MODEL, D_EXPERT), dsc),
        'moe_w2': _nrm(next(ks), (DEPTH, N_EXPERTS, D_EXPERT, D_MODEL), D_EXPERT ** -0.5 * DEEPNORM_BETA),
    }


def reference(x_prompt, x_sample, cache_mla_ckv, cache_mla_kpe, cache_swa_k, cache_swa_v, cache_ax_k, cache_ax_v,
              c, c_ctx, w_in, mla_q_norm, mla_kv_norm, mla_w_q_up, mla_w_kv_up, swa_sink, conv_w, conv_b,
              ax_q_norm, ax_k_norm, w_branch, w_out, w_mod, b_mod, ln1_g, ln1_b, ln2_g, ln2_b,
              moe_w_group, moe_b_group, moe_w_router, moe_b_router, moe_w1, moe_w3, moe_w2):
    pos = grid_positions(x_sample.shape[1])
    y_p, y_s = x_prompt, x_sample
    st_ckv, st_kpe, st_sk, st_sv, st_ak, st_av = [], [], [], [], [], []
    for l in range(DEPTH):
        p = {'w_in': w_in[l], 'mla_q_norm': mla_q_norm[l], 'mla_kv_norm': mla_kv_norm[l],
             'mla_w_q_up': mla_w_q_up[l], 'mla_w_kv_up': mla_w_kv_up[l], 'swa_sink': swa_sink[l],
             'conv_w': conv_w[l], 'conv_b': conv_b[l], 'ax_q_norm': ax_q_norm[l], 'ax_k_norm': ax_k_norm[l],
             'w_branch': w_branch[l], 'w_out': w_out[l], 'ln1_g': ln1_g[l], 'ln1_b': ln1_b[l],
             'ln2_g': ln2_g[l], 'ln2_b': ln2_b[l], 'moe_w_group': moe_w_group[l], 'moe_b_group': moe_b_group[l],
             'moe_w_router': moe_w_router[l], 'moe_b_router': moe_b_router[l],
             'moe_w1': moe_w1[l], 'moe_w3': moe_w3[l], 'moe_w2': moe_w2[l]}
        mods_ctx = (jax.nn.silu(c_ctx) @ w_mod[l] + b_mod[l])[None, None, :]
        mods_lat = (jax.nn.silu(c) @ w_mod[l] + b_mod[l])[:, None, :]
        y_p, (ckv, kpe, sk, sv, ak, av) = trunk_layer(y_p, mods_ctx, p, None, None)
        st_ckv.append(ckv)
        st_kpe.append(kpe)
        st_sk.append(sk)
        st_sv.append(sv)
        st_ak.append(ak)
        st_av.append(av)
        ctx_l = (cache_mla_ckv[:, l], cache_mla_kpe[:, l], cache_swa_k[:, l], cache_swa_v[:, l],
                 cache_ax_k[:, l], cache_ax_v[:, l])
        y_s, _ = trunk_layer(y_s, mods_lat, p, ctx_l, pos)
    new_mla_ckv = jnp.stack(st_ckv, axis=1)
    new_mla_kpe = jnp.stack(st_kpe, axis=1)
    new_swa_k = jnp.stack(st_sk, axis=1)
    new_swa_v = jnp.stack(st_sv, axis=1)
    new_ax_k = jnp.stack(st_ak, axis=1)
    new_ax_v = jnp.stack(st_av, axis=1)
    return (y_p, y_s, new_mla_ckv, new_mla_kpe, new_swa_k, new_swa_v, new_ax_k, new_ax_v)
```

```python
import functools

import jax
import jax.numpy as jnp
import numpy as np
from jax import lax
from jax.experimental import pallas as pl
from jax.experimental.pallas import tpu as pltpu

F32 = jnp.float32
BF16 = jnp.bfloat16
HIGHEST = lax.Precision.HIGHEST

D = 4096
DEPTH = 4
N_CTX_SEQ, T_CTX = 32, 256
N_LAT_SEQ, T_LAT = 2, 1024
PAST = 512
M_CTX = N_CTX_SEQ * T_CTX
M_LAT = N_LAT_SEQ * T_LAT
M = M_CTX + M_LAT
GRID_W = 64
ROPE_THETA = 10000.0
EPS = 1e-6
NEG_INF = -1e30

MLA_HEADS, MLA_NOPE, MLA_ROPE, MLA_V = 8, 128, 64, 128
MLA_Q_LORA, MLA_KV_LORA = 1024, 512
SWA_HEADS, SWA_KV, SWA_D, SWA_WINDOW = 16, 4, 64, 128
AX_HEADS, AX_KV, AX_D = 8, 4, 128
N_GROUPS, EPG, N_EXPERTS, TOP_K, D_EXPERT = 8, 8, 64, 2, 512
ALPHA = (2 * DEPTH) ** 0.25
MLA_SCALE = (MLA_NOPE + MLA_ROPE) ** -0.5
SWA_SCALE = SWA_D ** -0.5
AX_SCALE = AX_D ** -0.5

IN_DIM = 24640
Y2_START = 1600
Y2_COLS = IN_DIM - Y2_START
OFF_SQ, OFF_SK, OFF_SV = 0, 1024, 1280
OFF_CB, OFF_CC, OFF_CX = 1536, 2560, 3584
OFF_AQ, OFF_AK, OFF_AV = 4608, 5632, 6144
OFF_GATE = 6656

LANES = 128
VMEM_LIMIT = 56 * 1024 * 1024

TQ = 256
MOE_BM = 256
MOE_BLOCKS = (M * TOP_K) // MOE_BM + N_EXPERTS
MOE_ROWS = MOE_BLOCKS * MOE_BM


def _cp(sem):
    return pltpu.CompilerParams(dimension_semantics=sem, vmem_limit_bytes=VMEM_LIMIT)


def _mod_row(i, tm):
    n_ctx = M_CTX // tm
    per_lat = T_LAT // tm
    return jnp.where(i < n_ctx, 0, 1 + (i - n_ctx) // per_lat)


def _sigmoid(x):
    return 1.0 / (1.0 + jnp.exp(-x))


def _dot(a, b):
    return jnp.dot(a, b, preferred_element_type=F32)


def _dot_nt(a, b):
    return lax.dot_general(a, b, (((1,), (1,)), ((), ())), preferred_element_type=F32)


def _mods_kernel(c_ref, w_ref, b_ref, o_ref):
    c = c_ref[...]
    s = c * _sigmoid(c)
    o_ref[...] = jnp.dot(s, w_ref[...], precision=HIGHEST, preferred_element_type=F32) + b_ref[...]


def _mods(cc, w_mod, b_mod):
    tn = 1024
    n = 6 * D
    return pl.pallas_call(
        _mods_kernel,
        grid=(DEPTH, n // tn),
        in_specs=[
            pl.BlockSpec((8, D), lambda l, j: (0, 0)),
            pl.BlockSpec((None, D, tn), lambda l, j: (l, 0, j)),
            pl.BlockSpec((None, 1, tn), lambda l, j: (l, 0, j)),
        ],
        out_specs=pl.BlockSpec((None, 8, tn), lambda l, j: (l, 0, j)),
        out_shape=jax.ShapeDtypeStruct((DEPTH, 8, n), F32),
        compiler_params=_cp(("parallel", "parallel")),
        name="mods",
    )(cc, w_mod, b_mod.reshape(DEPTH, 1, n))


def _modulate_kernel(x_ref, sh_ref, sc_ref, o_ref):
    o_ref[...] = (x_ref[...] * (1.0 + sc_ref[...]) + sh_ref[...]).astype(BF16)


def _modulate(x, mods3):
    tm = 512
    return pl.pallas_call(
        _modulate_kernel,
        grid=(M // tm,),
        in_specs=[
            pl.BlockSpec((tm, D), lambda i: (i, 0)),
            pl.BlockSpec((None, 1, D), lambda i: (_mod_row(i, tm), 0, 0)),
            pl.BlockSpec((None, 1, D), lambda i: (_mod_row(i, tm), 0, 1)),
        ],
        out_specs=pl.BlockSpec((tm, D), lambda i: (i, 0)),
        out_shape=jax.ShapeDtypeStruct((M, D), BF16),
        compiler_params=_cp(("parallel",)),
        name="modulate",
    )(x, mods3, mods3)


def _mm_kernel(a_ref, w_ref, o_ref, wbf_ref):
    @pl.when(pl.program_id(1) == 0)
    def _():
        wbf_ref[...] = w_ref[...].astype(BF16)

    o_ref[...] = _dot(a_ref[...], wbf_ref[...]).astype(o_ref.dtype)


def _mm_shift_kernel(a_ref, w_ref, wn_ref, o_ref, wbf_ref):
    @pl.when(pl.program_id(1) == 0)
    def _():
        half = LANES // 2
        n_chunks = w_ref.shape[1] // LANES
        for c in range(n_chunks):
            lo = w_ref[:, c * LANES:(c + 1) * LANES]
            hi = w_ref[:, (c + 1) * LANES:(c + 2) * LANES] if c + 1 < n_chunks else wn_ref[...]
            lane = lax.broadcasted_iota(jnp.int32, lo.shape, 1)
            sel = jnp.where(lane < half, pltpu.roll(lo, half, 1), pltpu.roll(hi, half, 1))
            wbf_ref[:, c * LANES:(c + 1) * LANES] = sel.astype(BF16)

    o_ref[...] = _dot(a_ref[...], wbf_ref[...]).astype(o_ref.dtype)


def _mm(a, w, layer, col0, n_cols, tn, tm, out_dtype, name, shift=False):
    m_rows, k = a.shape
    assert n_cols % tn == 0 and col0 % tn == 0 and m_rows % tm == 0
    c0 = col0 // tn
    in_specs = [
        pl.BlockSpec((tm, k), lambda n, m: (m, 0)),
        pl.BlockSpec((None, k, tn), lambda n, m: (layer, 0, c0 + n)),
    ]
    args = [a, w]
    kern = _mm_kernel
    if shift:
        per = tn // LANES
        in_specs.append(pl.BlockSpec((None, k, LANES), lambda n, m: (layer, 0, (c0 + n + 1) * per)))
        args.append(w)
        kern = _mm_shift_kernel
    return pl.pallas_call(
        kern,
        grid=(n_cols // tn, m_rows // tm),
        in_specs=in_specs,
        out_specs=pl.BlockSpec((tm, tn), lambda n, m: (m, n)),
        out_shape=jax.ShapeDtypeStruct((m_rows, n_cols), out_dtype),
        scratch_shapes=[pltpu.VMEM((k, tn), BF16)],
        compiler_params=_cp(("parallel", "arbitrary")),
        name=name,
    )(*args)


def _rms(x, g):
    return x * lax.rsqrt(jnp.mean(x * x, axis=-1, keepdims=True) + EPS) * g


def _rope_chunk(x, cos, sin, quarter):
    lane = lax.broadcasted_iota(jnp.int32, x.shape, 1)
    even = ((lane // quarter) % 2) == 0
    partner = jnp.where(even, pltpu.roll(x, LANES - quarter, 1), pltpu.roll(x, quarter, 1))
    return x * cos + partner * sin


def _rope_wide(x, cos, sin, quarter):
    chunks = [_rope_chunk(x[:, c * LANES:(c + 1) * LANES], cos, sin, quarter) for c in range(x.shape[1] // LANES)]
    return chunks[0] if len(chunks) == 1 else jnp.concatenate(chunks, axis=1)


def _softmax_pv(s, v, sink=None):
    m = jnp.max(s, axis=-1, keepdims=True)
    if sink is not None:
        m = jnp.maximum(m, sink)
    p = jnp.exp(s - m)
    l = jnp.sum(p, axis=-1, keepdims=True)
    if sink is not None:
        l = l + jnp.exp(sink - m)
    return _dot(p.astype(BF16), v) / l


def _geom(lat):
    t = T_LAT if lat else T_CTX
    n_seq = N_LAT_SEQ if lat else N_CTX_SEQ
    kv_row0 = (M_CTX // t) if lat else 0
    q_row0 = (M_CTX // TQ) if lat else 0
    return t, n_seq, t // TQ, kv_row0, q_row0


def _mla_kernel(*refs, rope, cached, emit_cache):
    it = iter(refs)
    qlat_ref, ckv_ref, kpe_ref = next(it), next(it), next(it)
    cckv_ref, ckpe_ref = (next(it), next(it)) if cached else (None, None)
    wq_ref, wkv_ref, gq_ref, gkv_ref = next(it), next(it), next(it), next(it)
    cosq_ref, sinq_ref, cosk_ref, sink_ref = (next(it), next(it), next(it), next(it)) if rope else (None,) * 4
    o_ref = next(it)
    ckv_out, kpe_out = (next(it), next(it)) if emit_cache else (None, None)

    qn = _rms(qlat_ref[...], gq_ref[...]).astype(BF16)
    ckv = _rms(ckv_ref[...], gkv_ref[...])
    kpe = kpe_ref[...]
    if rope:
        kpe = _rope_chunk(kpe, cosk_ref[...], sink_ref[...], MLA_ROPE // 4)
    lane = lax.broadcasted_iota(jnp.int32, kpe.shape, 1)
    kpe = jnp.where(lane < MLA_ROPE, kpe, 0.0)
    if emit_cache:
        ckv_out[...] = ckv
        kpe_out[...] = kpe[:, :MLA_ROPE]
    if cached:
        ckv = jnp.concatenate([cckv_ref[...], ckv], axis=0)
        kpe = jnp.concatenate([ckpe_ref[...], kpe], axis=0)
    q = _dot(qn, wq_ref[...])
    kv = _dot(ckv.astype(BF16), wkv_ref[...]).astype(BF16)
    q_pe = q[:, LANES:]
    if rope:
        q_pe = _rope_chunk(q_pe, cosq_ref[...], sinq_ref[...], MLA_ROPE // 4)
    s = _dot_nt(q[:, :LANES].astype(BF16), kv[:, :LANES])
    s = (s + _dot_nt(q_pe.astype(BF16), kpe.astype(BF16))) * MLA_SCALE
    o_ref[...] = _softmax_pv(s, kv[:, LANES:]).astype(BF16)


def _mla(y1, ykpe, cache_ckv, cache_kpe, wq, wkv, gq, gkv, tabs, layer, lat):
    t, n_seq, nq, kv0, q0 = _geom(lat)
    hw = 2 * LANES
    in_specs = [
        pl.BlockSpec((TQ, MLA_Q_LORA), lambda b, i, h: (q0 + b * nq + i, 0)),
        pl.BlockSpec((t, MLA_KV_LORA), lambda b, i, h: (kv0 + b, MLA_Q_LORA // MLA_KV_LORA)),
        pl.BlockSpec((t, LANES), lambda b, i, h: (kv0 + b, 0)),
    ]
    args = [y1, y1, ykpe]
    if lat:
        in_specs += [
            pl.BlockSpec((None, None, PAST, MLA_KV_LORA), lambda b, i, h: (b, layer, 0, 0)),
            pl.BlockSpec((None, None, PAST, LANES), lambda b, i, h: (b, layer, 0, 0)),
        ]
        args += [cache_ckv, cache_kpe]
    in_specs += [
        pl.BlockSpec((None, MLA_Q_LORA, hw), lambda b, i, h: (layer, 0, h)),
        pl.BlockSpec((None, MLA_KV_LORA, hw), lambda b, i, h: (layer, 0, h)),
        pl.BlockSpec((None, 1, MLA_Q_LORA), lambda b, i, h: (layer, 0, 0)),
        pl.BlockSpec((None, 1, MLA_KV_LORA), lambda b, i, h: (layer, 0, 0)),
    ]
    args += [wq, wkv, gq, gkv]
    if lat:
        in_specs += [pl.BlockSpec((TQ, LANES), lambda b, i, h: (i, 0))] * 2
        in_specs += [pl.BlockSpec((t, LANES), lambda b, i, h: (0, 0))] * 2
        args += [tabs["mla_cos"], tabs["mla_sin"]] * 2
    out_specs = [pl.BlockSpec((TQ, MLA_V), lambda b, i, h: (b * nq + i, h))]
    out_shape = [jax.ShapeDtypeStruct((n_seq * t, MLA_HEADS * MLA_V), BF16)]
    if not lat:
        out_specs += [
            pl.BlockSpec((None, t, MLA_KV_LORA), lambda b, i, h: (b, 0, 0)),
            pl.BlockSpec((None, t, MLA_ROPE), lambda b, i, h: (b, 0, 0)),
        ]
        out_shape += [
            jax.ShapeDtypeStruct((n_seq, t, MLA_KV_LORA), F32),
            jax.ShapeDtypeStruct((n_seq, t, MLA_ROPE), F32),
        ]
    return pl.pallas_call(
        functools.partial(_mla_kernel, rope=lat, cached=lat, emit_cache=not lat),
        grid=(n_seq, nq, MLA_HEADS),
        in_specs=in_specs,
        out_specs=out_specs,
        out_shape=out_shape,
        compiler_params=_cp(("parallel", "arbitrary", "arbitrary")),
        name="mla_lat" if lat else "mla_ctx",
    )(*args)


def _swa_kernel(*refs, rope, cached):
    it = iter(refs)
    sink_ref = next(it)
    q_ref, k_ref, v_ref = next(it), next(it), next(it)
    ck_ref, cv_ref = (next(it), next(it)) if cached else (None, None)
    rep_ref = next(it)
    cosq_ref, sinq_ref, cosk_ref, sink_tab_ref = (next(it), next(it), next(it), next(it)) if rope else (None,) * 4
    o_ref = next(it)

    g = pl.program_id(2)
    per_group = SWA_HEADS // SWA_KV
    q = q_ref[...]
    k = k_ref[...]
    v = v_ref[...]
    if rope:
        q = _rope_wide(q, cosq_ref[...], sinq_ref[...], SWA_D // 4)
        k = _rope_wide(k, cosk_ref[...], sink_tab_ref[...], SWA_D // 4)
    if cached:
        k = jnp.concatenate([ck_ref[...], k], axis=0)
        v = jnp.concatenate([cv_ref[...], v], axis=0)
    tq, gw = q.shape
    s_len = k.shape[0]
    k4 = _dot(k.astype(BF16), rep_ref[...]).astype(BF16)
    v4 = _dot(v.astype(BF16), rep_ref[...]).astype(BF16)
    if cached:
        qpos = pl.program_id(1) * tq + lax.broadcasted_iota(jnp.int32, (tq, s_len), 0)
        kpos = lax.broadcasted_iota(jnp.int32, (tq, s_len), 1) - PAST
        visible = (kpos < 0) | (jnp.abs(kpos - qpos) <= SWA_WINDOW)
    slot_q = lax.broadcasted_iota(jnp.int32, (tq, gw), 1) // SWA_D
    slot_v = lax.broadcasted_iota(jnp.int32, (s_len, gw), 1) // SWA_D
    acc = jnp.zeros((tq, gw), F32)
    for j in range(per_group):
        qm = jnp.where(slot_q == j, q, 0.0).astype(BF16)
        s = _dot_nt(qm, k4) * SWA_SCALE
        if cached:
            s = jnp.where(visible, s, NEG_INF)
        vm = jnp.where(slot_v == j, v4, jnp.zeros_like(v4))
        acc = acc + _softmax_pv(s, vm, sink=sink_ref[g * per_group + j])
    o_ref[...] = acc.astype(BF16)


def _swa(y2, cache_k, cache_v, sink, rep, tabs, layer, lat):
    t, n_seq, nq, kv0, q0 = _geom(lat)
    gw = SWA_HEADS // SWA_KV * SWA_D
    kvw = SWA_KV * SWA_D
    in_specs = [
        pl.BlockSpec(memory_space=pltpu.SMEM),
        pl.BlockSpec((TQ, gw), lambda b, i, g: (q0 + b * nq + i, OFF_SQ // gw + g)),
        pl.BlockSpec((t, kvw), lambda b, i, g: (kv0 + b, OFF_SK // kvw)),
        pl.BlockSpec((t, kvw), lambda b, i, g: (kv0 + b, OFF_SV // kvw)),
    ]
    args = [sink, y2, y2, y2]
    if lat:
        in_specs += [pl.BlockSpec((None, None, PAST, kvw), lambda b, i, g: (b, layer, 0, 0))] * 2
        args += [cache_k, cache_v]
    in_specs.append(pl.BlockSpec((kvw, gw), lambda b, i, g: (0, g)))
    args.append(rep)
    if lat:
        in_specs += [pl.BlockSpec((TQ, LANES), lambda b, i, g: (i, 0))] * 2
        in_specs += [pl.BlockSpec((t, LANES), lambda b, i, g: (0, 0))] * 2
        args += [tabs["h64_cos"], tabs["h64_sin"]] * 2
    return pl.pallas_call(
        functools.partial(_swa_kernel, rope=lat, cached=lat),
        grid=(n_seq, nq, SWA_KV),
        in_specs=in_specs,
        out_specs=pl.BlockSpec((TQ, gw), lambda b, i, g: (b * nq + i, g)),
        out_shape=jax.ShapeDtypeStruct((n_seq * t, SWA_HEADS * SWA_D), BF16),
        compiler_params=_cp(("parallel", "parallel", "parallel")),
        name="swa_lat" if lat else "swa_ctx",
    )(*args)


def _conv_kernel(cb_ref, cc_ref, cx_ref, w_ref, b_ref, o_ref):
    u = cc_ref[...] * cx_ref[...]
    t = u.shape[0]
    row = lax.broadcasted_iota(jnp.int32, u.shape, 0)
    prev = jnp.where(row == 0, 0.0, pltpu.roll(u, 1, 0))
    nxt = jnp.where(row == t - 1, 0.0, pltpu.roll(u, t - 1, 0))
    y = w_ref[0:1, :] * prev + w_ref[1:2, :] * u + w_ref[2:3, :] * nxt + b_ref[...]
    o_ref[...] = (cb_ref[...] * y).astype(BF16)


def _conv(y2, conv_w, conv_b, layer, lat):
    t, n_seq, _, kv0, _ = _geom(lat)
    cw = 512
    return pl.pallas_call(
        _conv_kernel,
        grid=(n_seq, 1024 // cw),
        in_specs=[
            pl.BlockSpec((t, cw), lambda b, j: (kv0 + b, OFF_CB // cw + j)),
            pl.BlockSpec((t, cw), lambda b, j: (kv0 + b, OFF_CC // cw + j)),
            pl.BlockSpec((t, cw), lambda b, j: (kv0 + b, OFF_CX // cw + j)),
            pl.BlockSpec((None, 3, cw), lambda b, j: (layer, 0, j)),
            pl.BlockSpec((None, 1, cw), lambda b, j: (layer, 0, j)),
        ],
        out_specs=pl.BlockSpec((t, cw), lambda b, j: (b, j)),
        out_shape=jax.ShapeDtypeStruct((n_seq * t, 1024), BF16),
        compiler_params=_cp(("parallel", "parallel")),
        name="conv_lat" if lat else "conv_ctx",
    )(y2, y2, y2, conv_w, conv_b)


def _ax_kernel(*refs, rope, cached, emit_cache):
    it = iter(refs)
    q_ref, k_ref, v_ref = next(it), next(it), next(it)
    ck_ref, cv_ref = (next(it), next(it)) if cached else (None, None)
    gq_ref, gk_ref = next(it), next(it)
    cosq_ref, sinq_ref, cosk_ref, sink_ref = (next(it), next(it), next(it), next(it)) if rope else (None,) * 4
    o_ref = next(it)
    k_out = next(it) if emit_cache else None

    k = _rms(k_ref[...], gk_ref[...])
    if emit_cache:
        k_out[...] = k
    if rope:
        k = _rope_chunk(k, cosk_ref[...], sink_ref[...], AX_D // 4)
    v = v_ref[...]
    if cached:
        k = jnp.concatenate([ck_ref[...], k], axis=0)
        v = jnp.concatenate([cv_ref[...], v], axis=0)
    k = k.astype(BF16)
    v = v.astype(BF16)
    for h in range(q_ref.shape[1] // AX_D):
        qh = _rms(q_ref[:, h * AX_D:(h + 1) * AX_D], gq_ref[...])
        if rope:
            qh = _rope_chunk(qh, cosq_ref[...], sinq_ref[...], AX_D // 4)
        s = _dot_nt(qh.astype(BF16), k) * AX_SCALE
        o_ref[:, h * AX_D:(h + 1) * AX_D] = _softmax_pv(s, v).astype(BF16)


def _ax(y2, cache_k, cache_v, gq, gk, tabs, layer, lat):
    t, n_seq, nq, kv0, q0 = _geom(lat)
    qw = AX_HEADS // AX_KV * AX_D
    in_specs = [
        pl.BlockSpec((TQ, qw), lambda b, i, g: (q0 + b * nq + i, OFF_AQ // qw + g)),
        pl.BlockSpec((t, AX_D), lambda b, i, g: (kv0 + b, OFF_AK // AX_D + g)),
        pl.BlockSpec((t, AX_D), lambda b, i, g: (kv0 + b, OFF_AV // AX_D + g)),
    ]
    args = [y2, y2, y2]
    if lat:
        in_specs += [pl.BlockSpec((None, None, PAST, AX_D), lambda b, i, g: (b, layer, 0, g))] * 2
        args += [cache_k, cache_v]
    in_specs += [pl.BlockSpec((None, 1, AX_D), lambda b, i, g: (layer, 0, 0))] * 2
    args += [gq, gk]
    if lat:
        in_specs += [pl.BlockSpec((TQ, LANES), lambda b, i, g: (i, 0))] * 2
        in_specs += [pl.BlockSpec((t, LANES), lambda b, i, g: (0, 0))] * 2
        args += [tabs["h128_cos"], tabs["h128_sin"]] * 2
    out_specs = [pl.BlockSpec((TQ, qw), lambda b, i, g: (b * nq + i, g))]
    out_shape = [jax.ShapeDtypeStruct((n_seq * t, AX_HEADS * AX_D), BF16)]
    if not lat:
        out_specs.append(pl.BlockSpec((None, t, AX_D), lambda b, i, g: (b, 0, g)))
        out_shape.append(jax.ShapeDtypeStruct((n_seq, t, AX_KV * AX_D), F32))
    return pl.pallas_call(
        functools.partial(_ax_kernel, rope=lat, cached=lat, emit_cache=not lat),
        grid=(n_seq, nq, AX_KV),
        in_specs=in_specs,
        out_specs=out_specs,
        out_shape=out_shape,
        compiler_params=_cp(("parallel", "arbitrary", "arbitrary")),
        name="ax_lat" if lat else "ax_ctx",
    )(*args)


def _merge_kernel(b0, b1, b2, b3, w_ref, g0, g1, g2, g3, o_ref, wbf_ref):
    @pl.when(pl.program_id(1) == 0)
    def _():
        wbf_ref[...] = w_ref[...].astype(BF16)

    acc = _sigmoid(g0[...]) * _dot(b0[...], wbf_ref[0])
    acc = acc + _sigmoid(g1[...]) * _dot(b1[...], wbf_ref[1])
    acc = acc + _sigmoid(g2[...]) * _dot(b2[...], wbf_ref[2])
    acc = acc + _sigmoid(g3[...]) * _dot(b3[...], wbf_ref[3])
    o_ref[...] = acc.astype(BF16)


def _merge(branches, w_branch, y2, layer):
    tm, tn = 512, 512
    kb = 1024
    gate0 = OFF_GATE // tn
    per = D // tn
    b_spec = pl.BlockSpec((tm, kb), lambda n, m: (m, 0))
    g_specs = [pl.BlockSpec((tm, tn), functools.partial(lambda n, m, i: (m, gate0 + i * per + n), i=i))
               for i in range(4)]
    return pl.pallas_call(
        _merge_kernel,
        grid=(D // tn, M // tm),
        in_specs=[b_spec] * 4 + [pl.BlockSpec((None, 4, kb, tn), lambda n, m: (layer, 0, 0, n))] + g_specs,
        out_specs=pl.BlockSpec((tm, tn), lambda n, m: (m, n)),
        out_shape=jax.ShapeDtypeStruct((M, D), BF16),
        scratch_shapes=[pltpu.VMEM((4, kb, tn), BF16)],
        compiler_params=_cp(("parallel", "arbitrary")),
        name="merge",
    )(*branches, w_branch, y2, y2, y2, y2)


def _layer_norm(y, g, b):
    mu = jnp.mean(y, axis=-1, keepdims=True)
    yc = y - mu
    var = jnp.mean(yc * yc, axis=-1, keepdims=True)
    return yc * lax.rsqrt(var + EPS) * g + b


def _ln1_kernel(x_ref, mix_ref, gate_ref, sh_ref, sc_ref, g_ref, b_ref, wr_ref, br_ref,
                x1_ref, h2_ref, logit_ref):
    y = ALPHA * x_ref[...] + gate_ref[...] * mix_ref[...]
    x1 = _layer_norm(y, g_ref[...], b_ref[...])
    x1_ref[...] = x1
    h2 = x1 * (1.0 + sc_ref[...]) + sh_ref[...]
    h2_ref[...] = h2
    logit_ref[...] = jnp.dot(h2, wr_ref[...], precision=HIGHEST, preferred_element_type=F32) + br_ref[...]


def _ln1(x, mix, mods3, ln_g, ln_b, w_route, b_route, layer):
    tm = 256
    row = pl.BlockSpec((tm, D), lambda i: (i, 0))

    def mod(col):
        return pl.BlockSpec((None, 1, D), lambda i: (_mod_row(i, tm), 0, col))

    vec = pl.BlockSpec((None, 1, D), lambda i: (layer, 0, 0))
    return pl.pallas_call(
        _ln1_kernel,
        grid=(M // tm,),
        in_specs=[row, row, mod(2), mod(3), mod(4), vec, vec,
                  pl.BlockSpec((None, D, LANES), lambda i: (layer, 0, 0)),
                  pl.BlockSpec((None, 1, LANES), lambda i: (layer, 0, 0))],
        out_specs=[row, row, pl.BlockSpec((tm, LANES), lambda i: (i, 0))],
        out_shape=[jax.ShapeDtypeStruct((M, D), F32), jax.ShapeDtypeStruct((M, D), F32),
                   jax.ShapeDtypeStruct((M, LANES), F32)],
        compiler_params=_cp(("parallel",)),
        name="ln1",
    )(x, mix, mods3, mods3, mods3, ln_g, ln_b, w_route, b_route)


def _ln2_kernel(dest_ref, x_ref, y_hbm, gate_ref, g_ref, b_ref, sh_ref, sc_ref,
                x2_ref, hb_ref, zbuf, sem):
    tm = x_ref.shape[0]
    base = pl.program_id(0) * tm * TOP_K

    def issue(r, carry):
        for k in range(TOP_K):
            src = dest_ref[base + r * TOP_K + k]
            pltpu.make_async_copy(y_hbm.at[pl.ds(src, 1)], zbuf.at[pl.ds(k * tm + r, 1)], sem).start()
        return carry

    lax.fori_loop(0, tm, issue, 0)
    pltpu.make_async_copy(y_hbm.at[pl.ds(0, TOP_K * tm)], zbuf, sem).wait()
    ff = zbuf[0:tm, :] + zbuf[tm:2 * tm, :]
    y = ALPHA * x_ref[...] + gate_ref[...] * ff
    x2 = _layer_norm(y, g_ref[...], b_ref[...])
    x2_ref[...] = x2
    hb_ref[...] = (x2 * (1.0 + sc_ref[...]) + sh_ref[...]).astype(BF16)


def _ln2(x1, y_rows, dest, mods3, mods3_next, ln_g, ln_b, layer):
    tm = 256
    row = pl.BlockSpec((tm, D), lambda i, d: (i, 0))
    vec = pl.BlockSpec((None, 1, D), lambda i, d: (layer, 0, 0))
    grid_spec = pltpu.PrefetchScalarGridSpec(
        num_scalar_prefetch=1,
        grid=(M // tm,),
        in_specs=[row, pl.BlockSpec(memory_space=pl.ANY),
                  pl.BlockSpec((None, 1, D), lambda i, d: (_mod_row(i, tm), 0, 5)), vec, vec,
                  pl.BlockSpec((None, 1, D), lambda i, d: (_mod_row(i, tm), 0, 0)),
                  pl.BlockSpec((None, 1, D), lambda i, d: (_mod_row(i, tm), 0, 1))],
        out_specs=[row, row],
        scratch_shapes=[pltpu.VMEM((TOP_K * tm, D), F32), pltpu.SemaphoreType.DMA(())],
    )
    return pl.pallas_call(
        _ln2_kernel,
        grid_spec=grid_spec,
        out_shape=[jax.ShapeDtypeStruct((M, D), F32), jax.ShapeDtypeStruct((M, D), BF16)],
        compiler_params=_cp(("arbitrary",)),
        name="ln2",
    )(dest, x1, y_rows, mods3, ln_g, ln_b, mods3_next, mods3_next)


def _expert_changed(be_ref):
    b = pl.program_id(0)
    prev = be_ref[jnp.maximum(b - 1, 0)]
    return (b == 0) | (be_ref[b] != prev)


def _moe_up_kernel(be_ref, src_ref, h_hbm, w1_ref, w3_ref, o_ref, xbuf, w1_bf, w3_bf, sem):
    bm = xbuf.shape[0]
    base = pl.program_id(0) * bm

    def issue(r, carry):
        pltpu.make_async_copy(h_hbm.at[pl.ds(src_ref[base + r], 1)], xbuf.at[pl.ds(r, 1)], sem).start()
        return carry

    lax.fori_loop(0, bm, issue, 0)

    @pl.when(_expert_changed(be_ref))
    def _():
        w1_bf[...] = w1_ref[...].astype(BF16)
        w3_bf[...] = w3_ref[...].astype(BF16)

    pltpu.make_async_copy(h_hbm.at[pl.ds(0, bm)], xbuf, sem).wait()
    x = xbuf[...].astype(BF16)
    a = _dot(x, w1_bf[...])
    c = _dot(x, w3_bf[...])
    o_ref[...] = (a * _sigmoid(a) * c).astype(BF16)


def _moe_up(block_e, src_tok, h2, w1, w3, layer):
    grid_spec = pltpu.PrefetchScalarGridSpec(
        num_scalar_prefetch=2,
        grid=(MOE_BLOCKS,),
        in_specs=[pl.BlockSpec(memory_space=pl.ANY),
                  pl.BlockSpec((None, None, D, D_EXPERT), lambda b, be, src: (layer, be[b], 0, 0)),
                  pl.BlockSpec((None, None, D, D_EXPERT), lambda b, be, src: (layer, be[b], 0, 0))],
        out_specs=pl.BlockSpec((MOE_BM, D_EXPERT), lambda b, be, src: (b, 0)),
        scratch_shapes=[pltpu.VMEM((MOE_BM, D), F32), pltpu.VMEM((D, D_EXPERT), BF16),
                        pltpu.VMEM((D, D_EXPERT), BF16), pltpu.SemaphoreType.DMA(())],
    )
    return pl.pallas_call(
        _moe_up_kernel,
        grid_spec=grid_spec,
        out_shape=jax.ShapeDtypeStruct((MOE_ROWS, D_EXPERT), BF16),
        compiler_params=_cp(("arbitrary",)),
        name="moe_up",
    )(block_e, src_tok, h2, w1, w3)


def _moe_down_kernel(be_ref, h_ref, w2_ref, wrow_ref, o_ref, w2_bf):
    @pl.when(_expert_changed(be_ref))
    def _():
        w2_bf[...] = w2_ref[...].astype(BF16)

    o_ref[...] = _dot(h_ref[...], w2_bf[...]) * wrow_ref[...]


def _moe_down(block_e, hidden, w2, wrow, layer):
    grid_spec = pltpu.PrefetchScalarGridSpec(
        num_scalar_prefetch=1,
        grid=(MOE_BLOCKS,),
        in_specs=[pl.BlockSpec((MOE_BM, D_EXPERT), lambda b, be: (b, 0)),
                  pl.BlockSpec((None, None, D_EXPERT, D), lambda b, be: (layer, be[b], 0, 0)),
                  pl.BlockSpec((MOE_BM, 1), lambda b, be: (b, 0))],
        out_specs=pl.BlockSpec((MOE_BM, D), lambda b, be: (b, 0)),
        scratch_shapes=[pltpu.VMEM((D_EXPERT, D), BF16)],
    )
    return pl.pallas_call(
        _moe_down_kernel,
        grid_spec=grid_spec,
        out_shape=jax.ShapeDtypeStruct((MOE_ROWS, D), F32),
        compiler_params=_cp(("arbitrary",)),
        name="moe_down",
    )(block_e, hidden, w2, wrow)


def _route(logits):
    g_logits = logits[:, :N_GROUPS]
    e_logits = logits[:, N_GROUPS:N_GROUPS + N_EXPERTS].reshape(M, N_GROUPS, EPG)
    g_sel = jnp.argmax(g_logits, axis=-1).astype(jnp.int32)
    g_w = jnp.take_along_axis(jax.nn.softmax(g_logits, -1), g_sel[:, None], -1)
    e_sel = jnp.take_along_axis(e_logits, g_sel[:, None, None], axis=1)[:, 0]
    top_v, top_i = lax.top_k(e_sel, TOP_K)
    gate = g_w * jax.nn.softmax(top_v, -1)
    expert = g_sel[:, None] * EPG + top_i.astype(jnp.int32)
    n_assign = M * TOP_K
    e_flat = expert.reshape(n_assign)
    w_flat = gate.reshape(n_assign)
    tok = jnp.arange(n_assign, dtype=jnp.int32) // TOP_K
    order = jnp.argsort(e_flat)
    e_s = e_flat[order]
    counts = jnp.zeros((N_EXPERTS,), jnp.int32).at[e_flat].add(1)
    padded = (counts + MOE_BM - 1) // MOE_BM * MOE_BM
    start = jnp.cumsum(counts) - counts
    pend = jnp.cumsum(padded)
    pstart = pend - padded
    dest_s = pstart[e_s] + jnp.arange(n_assign, dtype=jnp.int32) - start[e_s]
    src_tok = jnp.zeros((MOE_ROWS,), jnp.int32).at[dest_s].set(tok[order])
    wrow = jnp.zeros((MOE_ROWS,), F32).at[dest_s].set(w_flat[order])
    dest = jnp.zeros((n_assign,), jnp.int32).at[order].set(dest_s)
    block_e = jnp.clip(jnp.searchsorted(pend, jnp.arange(MOE_BLOCKS, dtype=jnp.int32) * MOE_BM, side="right"),
                       0, N_EXPERTS - 1).astype(jnp.int32)
    return block_e, src_tok, wrow.reshape(MOE_ROWS, 1), dest


def _rope_tables():
    rows = T_LAT // GRID_W
    row = jnp.repeat(jnp.arange(rows, dtype=jnp.int32), GRID_W).astype(F32)
    col = jnp.tile(jnp.arange(GRID_W, dtype=jnp.int32), rows).astype(F32)

    def quarter_tables(head_dim):
        m = head_dim // 2
        inv = ROPE_THETA ** (-jnp.arange(0, m, 2, dtype=F32) / m)
        ar = row[:, None] * inv[None, :]
        ac = col[:, None] * inv[None, :]
        cos = jnp.concatenate([jnp.cos(ar), jnp.cos(ar), jnp.cos(ac), jnp.cos(ac)], -1)
        sin = jnp.concatenate([-jnp.sin(ar), jnp.sin(ar), -jnp.sin(ac), jnp.sin(ac)], -1)
        return cos, sin

    c64, s64 = quarter_tables(64)
    c128, s128 = quarter_tables(128)
    return {
        "h64_cos": jnp.concatenate([c64, c64], -1), "h64_sin": jnp.concatenate([s64, s64], -1),
        "h128_cos": c128, "h128_sin": s128,
        "mla_cos": jnp.concatenate([c64, jnp.ones_like(c64)], -1),
        "mla_sin": jnp.concatenate([s64, jnp.zeros_like(s64)], -1),
    }


def _replication_matrix():
    r = np.zeros((SWA_KV * SWA_D, SWA_HEADS * SWA_D), np.float32)
    for g in range(SWA_KV):
        for j in range(SWA_HEADS // SWA_KV):
            for c in range(SWA_D):
                r[g * SWA_D + c, g * 256 + j * SWA_D + c] = 1.0
    return jnp.asarray(r, BF16)


def kernel(x_prompt, x_sample, cache_mla_ckv, cache_mla_kpe, cache_swa_k, cache_swa_v, cache_ax_k, cache_ax_v,
           c, c_ctx, w_in, mla_q_norm, mla_kv_norm, mla_w_q_up, mla_w_kv_up, swa_sink, conv_w, conv_b,
           ax_q_norm, ax_k_norm, w_branch, w_out, w_mod, b_mod, ln1_g, ln1_b, ln2_g, ln2_b,
           moe_w_group, moe_b_group, moe_w_router, moe_b_router, moe_w1, moe_w3, moe_w2):
    x = jnp.concatenate([x_prompt.reshape(M_CTX, D), x_sample.reshape(M_LAT, D)], axis=0)
    cc = jnp.concatenate([c_ctx[None, :], c, jnp.zeros((8 - 1 - N_LAT_SEQ, D), F32)], axis=0)
    mods = _mods(cc, w_mod, b_mod)
    mods3 = [mods[l].reshape(8, 1, 6 * D) for l in range(DEPTH)]

    tabs = _rope_tables()
    rep = _replication_matrix()
    wq = jnp.pad(mla_w_q_up.reshape(DEPTH, MLA_Q_LORA, MLA_HEADS, MLA_NOPE + MLA_ROPE),
                 ((0, 0), (0, 0), (0, 0), (0, 2 * LANES - MLA_NOPE - MLA_ROPE))
                 ).reshape(DEPTH, MLA_Q_LORA, MLA_HEADS * 2 * LANES).astype(BF16)
    wkv = mla_w_kv_up.astype(BF16)
    cache_kpe = jnp.pad(cache_mla_kpe, ((0, 0), (0, 0), (0, 0), (0, LANES - MLA_ROPE)))
    c_sk = cache_swa_k.reshape(N_LAT_SEQ, DEPTH, PAST, SWA_KV * SWA_D)
    c_sv = cache_swa_v.reshape(N_LAT_SEQ, DEPTH, PAST, SWA_KV * SWA_D)
    c_ak = cache_ax_k.reshape(N_LAT_SEQ, DEPTH, PAST, AX_KV * AX_D)
    c_av = cache_ax_v.reshape(N_LAT_SEQ, DEPTH, PAST, AX_KV * AX_D)
    gq3 = mla_q_norm.reshape(DEPTH, 1, MLA_Q_LORA)
    gkv3 = mla_kv_norm.reshape(DEPTH, 1, MLA_KV_LORA)
    axq3 = ax_q_norm.reshape(DEPTH, 1, AX_D)
    axk3 = ax_k_norm.reshape(DEPTH, 1, AX_D)
    conv_b3 = conv_b.reshape(DEPTH, 1, 1024)
    ln1g, ln1b = ln1_g.reshape(DEPTH, 1, D), ln1_b.reshape(DEPTH, 1, D)
    ln2g, ln2b = ln2_g.reshape(DEPTH, 1, D), ln2_b.reshape(DEPTH, 1, D)
    n_route = N_GROUPS + N_EXPERTS
    w_route = jnp.pad(jnp.concatenate([moe_w_group, moe_w_router], -1), ((0, 0), (0, 0), (0, LANES - n_route)))
    b_route = jnp.pad(jnp.concatenate([moe_b_group, moe_b_router], -1), ((0, 0), (0, LANES - n_route))
                      ).reshape(DEPTH, 1, LANES)

    hb = _modulate(x, mods3[0])
    st = {k: [] for k in ("ckv", "kpe", "sk", "sv", "ak", "av")}
    for l in range(DEPTH):
        y1 = _mm(hb, w_in, l, 0, 1536, 512, 1024, F32, "w_in_a")
        ykpe = _mm(hb, w_in, l, 1536, LANES, LANES, 1024, F32, "w_in_kpe")
        y2 = _mm(hb, w_in, l, 1536, Y2_COLS, 512, 1024, F32, "w_in_b", shift=True)

        oa_c, ckv_c, kpe_c = _mla(y1, ykpe, None, None, wq, wkv, gq3, gkv3, tabs, l, lat=False)
        (oa_l,) = _mla(y1, ykpe, cache_mla_ckv, cache_kpe, wq, wkv, gq3, gkv3, tabs, l, lat=True)
        ob_c = _swa(y2, None, None, swa_sink[l], rep, tabs, l, lat=False)
        ob_l = _swa(y2, c_sk, c_sv, swa_sink[l], rep, tabs, l, lat=True)
        oc_c = _conv(y2, conv_w, conv_b3, l, lat=False)
        oc_l = _conv(y2, conv_w, conv_b3, l, lat=True)
        od_c, ak_c = _ax(y2, None, None, axq3, axk3, tabs, l, lat=False)
        (od_l,) = _ax(y2, c_ak, c_av, axq3, axk3, tabs, l, lat=True)
        branches = [jnp.concatenate(p, axis=0) for p in ((oa_c, oa_l), (ob_c, ob_l), (oc_c, oc_l), (od_c, od_l))]

        st["ckv"].append(ckv_c)
        st["kpe"].append(kpe_c)
        st["sk"].append(y2[:M_CTX, OFF_SK:OFF_SK + 256].reshape(N_CTX_SEQ, T_CTX, SWA_KV, SWA_D))
        st["sv"].append(y2[:M_CTX, OFF_SV:OFF_SV + 256].reshape(N_CTX_SEQ, T_CTX, SWA_KV, SWA_D))
        st["ak"].append(ak_c.reshape(N_CTX_SEQ, T_CTX, AX_KV, AX_D))
        st["av"].append(y2[:M_CTX, OFF_AV:OFF_AV + 512].reshape(N_CTX_SEQ, T_CTX, AX_KV, AX_D))

        merged = _merge(branches, w_branch, y2, l)
        mix = _mm(merged, w_out, l, 0, D, 512, 1024, F32, "w_out")
        x1, h2, logits = _ln1(x, mix, mods3[l], ln1g, ln1b, w_route, b_route, l)

        block_e, src_tok, wrow, dest = _route(logits)
        hidden = _moe_up(block_e, src_tok, h2, moe_w1, moe_w3, l)
        y_rows = _moe_down(block_e, hidden, moe_w2, wrow, l)
        x, hb = _ln2(x1, y_rows, dest, mods3[l], mods3[min(l + 1, DEPTH - 1)], ln2g, ln2b, l)

    y_p = x[:M_CTX].reshape(N_CTX_SEQ, T_CTX, D)
    y_s = x[M_CTX:].reshape(N_LAT_SEQ, T_LAT, D)
    return (y_p, y_s,
            jnp.stack(st["ckv"], axis=1), jnp.stack(st["kpe"], axis=1),
            jnp.stack(st["sk"], axis=1), jnp.stack(st["sv"], axis=1),
            jnp.stack(st["ak"], axis=1), jnp.stack(st["av"], axis=1))
```

```python
import functools

import jax
import jax.numpy as jnp
import numpy as np
from jax import lax
from jax.experimental import pallas as pl
from jax.experimental.pallas import tpu as pltpu

F32 = jnp.float32
BF16 = jnp.bfloat16
I32 = jnp.int32
HIGHEST = lax.Precision.HIGHEST

D = 4096
DEPTH = 4
N_CTX_SEQ, T_CTX = 32, 256
N_LAT_SEQ, T_LAT = 2, 1024
PAST = 512
M_CTX = N_CTX_SEQ * T_CTX
M_LAT = N_LAT_SEQ * T_LAT
M = M_CTX + M_LAT
GRID_W = 64
ROPE_THETA = 10000.0
EPS = 1e-6
NEG_INF = -1e30

MLA_HEADS, MLA_NOPE, MLA_ROPE, MLA_V = 8, 128, 64, 128
MLA_Q_LORA, MLA_KV_LORA = 1024, 512
SWA_HEADS, SWA_KV, SWA_D, SWA_WINDOW = 16, 4, 64, 128
AX_HEADS, AX_KV, AX_D = 8, 4, 128
BRANCH_DIM = 1024
N_GROUPS, EPG, N_EXPERTS, TOP_K, D_EXPERT = 8, 8, 64, 2, 512
ALPHA = (2 * DEPTH) ** 0.25
MLA_SCALE = (MLA_NOPE + MLA_ROPE) ** -0.5
SWA_SCALE = SWA_D ** -0.5
AX_SCALE = AX_D ** -0.5

IN_DIM = 24640
Y2_START = 1600
Y2_COLS = IN_DIM - Y2_START
OFF_SQ, OFF_SK, OFF_SV = 0, 1024, 1280
OFF_CB, OFF_CC, OFF_CX = 1536, 2560, 3584
OFF_AQ, OFF_AK, OFF_AV = 4608, 5632, 6144
OFF_GATE = 6656

LANES = 128
VMEM_LIMIT = 56 * 1024 * 1024

TQ = 256
N_ASSIGN = M * TOP_K
MOE_BM = 256
MOE_BLOCKS = N_ASSIGN // MOE_BM + N_EXPERTS
MOE_ROWS = MOE_BLOCKS * MOE_BM
DMA_UNROLL = 8


def _cp(sem):
    return pltpu.CompilerParams(dimension_semantics=sem, vmem_limit_bytes=VMEM_LIMIT)


def _mod_row(i, tm):
    n_ctx = M_CTX // tm
    per_lat = T_LAT // tm
    return jnp.where(i < n_ctx, 0, 1 + (i - n_ctx) // per_lat)


def _sigmoid(x):
    return 1.0 / (1.0 + jnp.exp(-x))


def _dot(a, b):
    return jnp.dot(a, b, preferred_element_type=F32)


def _dot_nt(a, b):
    return lax.dot_general(a, b, (((1,), (1,)), ((), ())), preferred_element_type=F32)


def _mods_kernel(c_ref, w_ref, b_ref, o_ref):
    c = c_ref[...]
    s = c * _sigmoid(c)
    o_ref[...] = jnp.dot(s, w_ref[...], precision=HIGHEST, preferred_element_type=F32) + b_ref[...]


def _mods(cc, w_mod, b_mod):
    tn = 1024
    n = 6 * D
    return pl.pallas_call(
        _mods_kernel,
        grid=(DEPTH, n // tn),
        in_specs=[
            pl.BlockSpec((8, D), lambda l, j: (0, 0)),
            pl.BlockSpec((None, D, tn), lambda l, j: (l, 0, j)),
            pl.BlockSpec((None, 1, tn), lambda l, j: (l, 0, j)),
        ],
        out_specs=pl.BlockSpec((None, 8, tn), lambda l, j: (l, 0, j)),
        out_shape=jax.ShapeDtypeStruct((DEPTH, 8, n), F32),
        compiler_params=_cp(("parallel", "parallel")),
        name="mods",
    )(cc, w_mod, b_mod.reshape(DEPTH, 1, n))


def _modulate_kernel(x_ref, sh_ref, sc_ref, o_ref):
    o_ref[...] = (x_ref[...] * (1.0 + sc_ref[...]) + sh_ref[...]).astype(BF16)


def _modulate(x, mods3):
    tm = 512
    return pl.pallas_call(
        _modulate_kernel,
        grid=(M // tm,),
        in_specs=[
            pl.BlockSpec((tm, D), lambda i: (i, 0)),
            pl.BlockSpec((None, 1, D), lambda i: (_mod_row(i, tm), 0, 0)),
            pl.BlockSpec((None, 1, D), lambda i: (_mod_row(i, tm), 0, 1)),
        ],
        out_specs=pl.BlockSpec((tm, D), lambda i: (i, 0)),
        out_shape=jax.ShapeDtypeStruct((M, D), BF16),
        compiler_params=_cp(("parallel",)),
        name="modulate",
    )(x, mods3, mods3)


def _mm_kernel(a_ref, w_ref, o_ref, wbf_ref):
    @pl.when(pl.program_id(1) == 0)
    def _():
        wbf_ref[...] = w_ref[...].astype(BF16)

    o_ref[...] = _dot(a_ref[...], wbf_ref[...]).astype(o_ref.dtype)


def _mm(a, w, layer, n_cols, tn, tm, out_dtype, name):
    m_rows, k = a.shape
    return pl.pallas_call(
        _mm_kernel,
        grid=(n_cols // tn, m_rows // tm),
        in_specs=[pl.BlockSpec((tm, k), lambda n, m: (m, 0)),
                  pl.BlockSpec((None, k, tn), lambda n, m: (layer, 0, n))],
        out_specs=pl.BlockSpec((tm, tn), lambda n, m: (m, n)),
        out_shape=jax.ShapeDtypeStruct((m_rows, n_cols), out_dtype),
        scratch_shapes=[pltpu.VMEM((k, tn), BF16)],
        compiler_params=_cp(("parallel", "arbitrary")),
        name=name,
    )(a, w)


def _mm_t_kernel(a_ref, w_ref, o_ref, wbf_ref):
    @pl.when(pl.program_id(1) == 0)
    def _():
        wbf_ref[...] = w_ref[...].astype(BF16)

    o_ref[...] = _dot_nt(a_ref[...], wbf_ref[...]).astype(o_ref.dtype)


def _mm_t_shift_kernel(a_ref, w_ref, wn_ref, o_ref, wbf_ref):
    @pl.when(pl.program_id(1) == 0)
    def _():
        half = LANES // 2
        tn = w_ref.shape[0]
        wbf_ref[0:tn - half, :] = w_ref[half:tn, :].astype(BF16)
        wbf_ref[tn - half:tn, :] = wn_ref[0:half, :].astype(BF16)

    o_ref[...] = _dot_nt(a_ref[...], wbf_ref[...]).astype(o_ref.dtype)


def _mm_t(a, wt, layer, row0, n_cols, tn, tm, out_dtype, name, shift=False):
    m_rows, k = a.shape
    assert n_cols % tn == 0 and row0 % tn == 0 and m_rows % tm == 0
    r0 = row0 // tn
    in_specs = [
        pl.BlockSpec((tm, k), lambda n, m: (m, 0)),
        pl.BlockSpec((None, tn, k), lambda n, m: (layer, r0 + n, 0)),
    ]
    args = [a, wt]
    kern = _mm_t_kernel
    if shift:
        per = tn // LANES
        in_specs.append(pl.BlockSpec((None, LANES, k), lambda n, m: (layer, (r0 + n + 1) * per, 0)))
        args.append(wt)
        kern = _mm_t_shift_kernel
    return pl.pallas_call(
        kern,
        grid=(n_cols // tn, m_rows // tm),
        in_specs=in_specs,
        out_specs=pl.BlockSpec((tm, tn), lambda n, m: (m, n)),
        out_shape=jax.ShapeDtypeStruct((m_rows, n_cols), out_dtype),
        scratch_shapes=[pltpu.VMEM((tn, k), BF16)],
        compiler_params=_cp(("parallel", "arbitrary")),
        name=name,
    )(*args)


def _rms(x, g):
    return x * lax.rsqrt(jnp.mean(x * x, axis=-1, keepdims=True) + EPS) * g


def _rope_chunk(x, cos, sin, quarter):
    lane = lax.broadcasted_iota(I32, x.shape, 1)
    even = ((lane // quarter) % 2) == 0
    partner = jnp.where(even, pltpu.roll(x, LANES - quarter, 1), pltpu.roll(x, quarter, 1))
    return x * cos + partner * sin


def _rope_wide(x, cos, sin, quarter):
    chunks = [_rope_chunk(x[:, c * LANES:(c + 1) * LANES], cos, sin, quarter) for c in range(x.shape[1] // LANES)]
    return chunks[0] if len(chunks) == 1 else jnp.concatenate(chunks, axis=1)


def _softmax_pv(s, v, sink=None):
    m = jnp.max(s, axis=-1, keepdims=True)
    if sink is not None:
        m = jnp.maximum(m, sink)
    p = jnp.exp(s - m)
    l = jnp.sum(p, axis=-1, keepdims=True)
    if sink is not None:
        l = l + jnp.exp(sink - m)
    return _dot(p.astype(BF16), v) / l


def _geom(lat):
    t = T_LAT if lat else T_CTX
    n_seq = N_LAT_SEQ if lat else N_CTX_SEQ
    kv_row0 = (M_CTX // t) if lat else 0
    q_row0 = (M_CTX // TQ) if lat else 0
    return t, n_seq, t // TQ, kv_row0, q_row0


def _branch_out(prev, in_specs, args):
    aliases = {len(args): 0}
    in_specs.append(pl.BlockSpec(memory_space=pl.ANY))
    args.append(prev)
    return aliases


def _mla_kernel(*refs, rope, cached, emit_cache):
    it = iter(refs)
    qlat_ref, ckv_ref, kpe_ref = next(it), next(it), next(it)
    cckv_ref, ckpe_ref = (next(it), next(it)) if cached else (None, None)
    wq_ref, wkv_ref, gq_ref, gkv_ref = next(it), next(it), next(it), next(it)
    cosq_ref, sinq_ref, cosk_ref, sink_ref = (next(it), next(it), next(it), next(it)) if rope else (None,) * 4
    next(it)
    o_ref = next(it)
    ckv_out, kpe_out = (next(it), next(it)) if emit_cache else (None, None)

    qn = _rms(qlat_ref[...], gq_ref[...]).astype(BF16)
    ckv = _rms(ckv_ref[...], gkv_ref[...])
    kpe = kpe_ref[...]
    if rope:
        kpe = _rope_chunk(kpe, cosk_ref[...], sink_ref[...], MLA_ROPE // 4)
    lane = lax.broadcasted_iota(I32, kpe.shape, 1)
    kpe = jnp.where(lane < MLA_ROPE, kpe, 0.0)
    if emit_cache:
        ckv_out[...] = ckv
        kpe_out[...] = kpe[:, :MLA_ROPE]
    if cached:
        ckv = jnp.concatenate([cckv_ref[...], ckv], axis=0)
        kpe = jnp.concatenate([ckpe_ref[...], kpe], axis=0)
    ckv = ckv.astype(BF16)
    kpe = kpe.astype(BF16)
    hw = 2 * LANES
    for h in range(wq_ref.shape[1] // hw):
        q = _dot(qn, wq_ref[:, h * hw:(h + 1) * hw])
        kv = _dot(ckv, wkv_ref[:, h * hw:(h + 1) * hw]).astype(BF16)
        q_pe = q[:, LANES:]
        if rope:
            q_pe = _rope_chunk(q_pe, cosq_ref[...], sinq_ref[...], MLA_ROPE // 4)
        s = _dot_nt(q[:, :LANES].astype(BF16), kv[:, :LANES])
        s = (s + _dot_nt(q_pe.astype(BF16), kpe)) * MLA_SCALE
        o_ref[:, h * MLA_V:(h + 1) * MLA_V] = _softmax_pv(s, kv[:, LANES:]).astype(BF16)


def _mla(y1, ykpe, cache_ckv, cache_kpe, wq, wkv, gq, gkv, tabs, layer, lat, prev=None):
    t, n_seq, nq, kv0, q0 = _geom(lat)
    hp = 1 if lat else MLA_HEADS
    hw = 2 * LANES * hp
    in_specs = [
        pl.BlockSpec((TQ, MLA_Q_LORA), lambda b, i, h: (q0 + b * nq + i, 0)),
        pl.BlockSpec((t, MLA_KV_LORA), lambda b, i, h: (kv0 + b, MLA_Q_LORA // MLA_KV_LORA)),
        pl.BlockSpec((t, LANES), lambda b, i, h: (kv0 + b, 0)),
    ]
    args = [y1, y1, ykpe]
    if lat:
        in_specs += [
            pl.BlockSpec((None, None, PAST, MLA_KV_LORA), lambda b, i, h: (b, layer, 0, 0)),
            pl.BlockSpec((None, None, PAST, LANES), lambda b, i, h: (b, layer, 0, 0)),
        ]
        args += [cache_ckv, cache_kpe]
    in_specs += [
        pl.BlockSpec((None, MLA_Q_LORA, hw), lambda b, i, h: (layer, 0, h)),
        pl.BlockSpec((None, MLA_KV_LORA, hw), lambda b, i, h: (layer, 0, h)),
        pl.BlockSpec((None, 1, MLA_Q_LORA), lambda b, i, h: (layer, 0, 0)),
        pl.BlockSpec((None, 1, MLA_KV_LORA), lambda b, i, h: (layer, 0, 0)),
    ]
    args += [wq, wkv, gq, gkv]
    if lat:
        in_specs += [pl.BlockSpec((TQ, LANES), lambda b, i, h: (i, 0))] * 2
        in_specs += [pl.BlockSpec((t, LANES), lambda b, i, h: (0, 0))] * 2
        args += [tabs["mla_cos"], tabs["mla_sin"]] * 2
    aliases = _branch_out(prev, in_specs, args)
    out_specs = [pl.BlockSpec((TQ, MLA_V * hp), lambda b, i, h: (q0 + b * nq + i, h))]
    out_shape = [jax.ShapeDtypeStruct((M, BRANCH_DIM), BF16)]
    if not lat:
        out_specs += [
            pl.BlockSpec((None, t, MLA_KV_LORA), lambda b, i, h: (b, 0, 0)),
            pl.BlockSpec((None, t, MLA_ROPE), lambda b, i, h: (b, 0, 0)),
        ]
        out_shape += [
            jax.ShapeDtypeStruct((n_seq, t, MLA_KV_LORA), F32),
            jax.ShapeDtypeStruct((n_seq, t, MLA_ROPE), F32),
        ]
    return pl.pallas_call(
        functools.partial(_mla_kernel, rope=lat, cached=lat, emit_cache=not lat),
        grid=(n_seq, nq, MLA_HEADS // hp),
        in_specs=in_specs,
        out_specs=out_specs,
        out_shape=out_shape,
        input_output_aliases=aliases,
        compiler_params=_cp(("parallel", "arbitrary", "arbitrary")),
        name="mla_lat" if lat else "mla_ctx",
    )(*args)


def _swa_kernel(*refs, rope, cached):
    it = iter(refs)
    sink_ref = next(it)
    q_ref, k_ref, v_ref = next(it), next(it), next(it)
    ck_ref, cv_ref = (next(it), next(it)) if cached else (None, None)
    rep_ref = next(it)
    cosq_ref, sinq_ref, cosk_ref, sink_tab_ref = (next(it), next(it), next(it), next(it)) if rope else (None,) * 4
    next(it)
    o_ref = next(it)

    per_group = SWA_HEADS // SWA_KV
    gw = per_group * SWA_D
    n_groups = q_ref.shape[1] // gw
    g0 = pl.program_id(2) * n_groups
    k = k_ref[...]
    v = v_ref[...]
    if rope:
        k = _rope_wide(k, cosk_ref[...], sink_tab_ref[...], SWA_D // 4)
    if cached:
        k = jnp.concatenate([ck_ref[...], k], axis=0)
        v = jnp.concatenate([cv_ref[...], v], axis=0)
    k = k.astype(BF16)
    v = v.astype(BF16)
    tq = q_ref.shape[0]
    s_len = k.shape[0]
    if cached:
        qpos = pl.program_id(1) * tq + lax.broadcasted_iota(I32, (tq, s_len), 0)
        kpos = lax.broadcasted_iota(I32, (tq, s_len), 1) - PAST
        visible = (kpos < 0) | (jnp.abs(kpos - qpos) <= SWA_WINDOW)
    slot_q = lax.broadcasted_iota(I32, (tq, gw), 1) // SWA_D
    slot_v = lax.broadcasted_iota(I32, (s_len, gw), 1) // SWA_D
    for gi in range(n_groups):
        q = q_ref[:, gi * gw:(gi + 1) * gw]
        if rope:
            q = _rope_wide(q, cosq_ref[...], sinq_ref[...], SWA_D // 4)
        k4 = _dot(k, rep_ref[:, gi * gw:(gi + 1) * gw]).astype(BF16)
        v4 = _dot(v, rep_ref[:, gi * gw:(gi + 1) * gw]).astype(BF16)
        acc = jnp.zeros((tq, gw), F32)
        for j in range(per_group):
            qm = jnp.where(slot_q == j, q, 0.0).astype(BF16)
            s = _dot_nt(qm, k4) * SWA_SCALE
            if cached:
                s = jnp.where(visible, s, NEG_INF)
            vm = jnp.where(slot_v == j, v4, jnp.zeros_like(v4))
            acc = acc + _softmax_pv(s, vm, sink=sink_ref[(g0 + gi) * per_group + j])
        o_ref[:, gi * gw:(gi + 1) * gw] = acc.astype(BF16)


def _swa(y2, cache_k, cache_v, sink, rep, tabs, layer, lat, prev=None):
    t, n_seq, nq, kv0, q0 = _geom(lat)
    gp = 1 if lat else SWA_KV
    gw = SWA_HEADS // SWA_KV * SWA_D * gp
    kvw = SWA_KV * SWA_D
    in_specs = [
        pl.BlockSpec(memory_space=pltpu.SMEM),
        pl.BlockSpec((TQ, gw), lambda b, i, g: (q0 + b * nq + i, OFF_SQ // gw + g)),
        pl.BlockSpec((t, kvw), lambda b, i, g: (kv0 + b, OFF_SK // kvw)),
        pl.BlockSpec((t, kvw), lambda b, i, g: (kv0 + b, OFF_SV // kvw)),
    ]
    args = [sink, y2, y2, y2]
    if lat:
        in_specs += [pl.BlockSpec((None, None, PAST, kvw), lambda b, i, g: (b, layer, 0, 0))] * 2
        args += [cache_k, cache_v]
    in_specs.append(pl.BlockSpec((kvw, gw), lambda b, i, g: (0, g)))
    args.append(rep)
    if lat:
        in_specs += [pl.BlockSpec((TQ, LANES), lambda b, i, g: (i, 0))] * 2
        in_specs += [pl.BlockSpec((t, LANES), lambda b, i, g: (0, 0))] * 2
        args += [tabs["h64_cos"], tabs["h64_sin"]] * 2
    aliases = _branch_out(prev, in_specs, args)
    return pl.pallas_call(
        functools.partial(_swa_kernel, rope=lat, cached=lat),
        grid=(n_seq, nq, SWA_KV // gp),
        in_specs=in_specs,
        out_specs=pl.BlockSpec((TQ, gw), lambda b, i, g: (q0 + b * nq + i, g)),
        out_shape=jax.ShapeDtypeStruct((M, BRANCH_DIM), BF16),
        input_output_aliases=aliases,
        compiler_params=_cp(("parallel", "parallel", "parallel")),
        name="swa_lat" if lat else "swa_ctx",
    )(*args)


def _conv_kernel(cb_ref, cc_ref, cx_ref, w_ref, b_ref, *rest):
    o_ref = rest[-1]
    u = cc_ref[...] * cx_ref[...]
    t = u.shape[0]
    row = lax.broadcasted_iota(I32, u.shape, 0)
    prev = jnp.where(row == 0, 0.0, pltpu.roll(u, 1, 0))
    nxt = jnp.where(row == t - 1, 0.0, pltpu.roll(u, t - 1, 0))
    y = w_ref[0:1, :] * prev + w_ref[1:2, :] * u + w_ref[2:3, :] * nxt + b_ref[...]
    o_ref[...] = (cb_ref[...] * y).astype(BF16)


def _conv(y2, conv_w, conv_b, layer, lat, prev=None):
    t, n_seq, _, kv0, _ = _geom(lat)
    cw = 512
    in_specs = [
        pl.BlockSpec((t, cw), lambda b, j: (kv0 + b, OFF_CB // cw + j)),
        pl.BlockSpec((t, cw), lambda b, j: (kv0 + b, OFF_CC // cw + j)),
        pl.BlockSpec((t, cw), lambda b, j: (kv0 + b, OFF_CX // cw + j)),
        pl.BlockSpec((None, 3, cw), lambda b, j: (layer, 0, j)),
        pl.BlockSpec((None, 1, cw), lambda b, j: (layer, 0, j)),
    ]
    args = [y2, y2, y2, conv_w, conv_b]
    aliases = _branch_out(prev, in_specs, args)
    return pl.pallas_call(
        _conv_kernel,
        grid=(n_seq, BRANCH_DIM // cw),
        in_specs=in_specs,
        out_specs=pl.BlockSpec((t, cw), lambda b, j: (kv0 + b, j)),
        out_shape=jax.ShapeDtypeStruct((M, BRANCH_DIM), BF16),
        input_output_aliases=aliases,
        compiler_params=_cp(("parallel", "parallel")),
        name="conv_lat" if lat else "conv_ctx",
    )(*args)


def _ax_kernel(*refs, rope, cached, emit_cache):
    it = iter(refs)
    q_ref, k_ref, v_ref = next(it), next(it), next(it)
    ck_ref, cv_ref = (next(it), next(it)) if cached else (None, None)
    gq_ref, gk_ref = next(it), next(it)
    cosq_ref, sinq_ref, cosk_ref, sink_ref = (next(it), next(it), next(it), next(it)) if rope else (None,) * 4
    next(it)
    o_ref = next(it)
    k_out = next(it) if emit_cache else None

    per_kv = AX_HEADS // AX_KV
    for g in range(k_ref.shape[1] // AX_D):
        k = _rms(k_ref[:, g * AX_D:(g + 1) * AX_D], gk_ref[...])
        if emit_cache:
            k_out[:, g * AX_D:(g + 1) * AX_D] = k
        if rope:
            k = _rope_chunk(k, cosk_ref[...], sink_ref[...], AX_D // 4)
        v = v_ref[:, g * AX_D:(g + 1) * AX_D]
        if cached:
            k = jnp.concatenate([ck_ref[:, g * AX_D:(g + 1) * AX_D], k], axis=0)
            v = jnp.concatenate([cv_ref[:, g * AX_D:(g + 1) * AX_D], v], axis=0)
        k = k.astype(BF16)
        v = v.astype(BF16)
        for h in range(g * per_kv, (g + 1) * per_kv):
            qh = _rms(q_ref[:, h * AX_D:(h + 1) * AX_D], gq_ref[...])
            if rope:
                qh = _rope_chunk(qh, cosq_ref[...], sinq_ref[...], AX_D // 4)
            s = _dot_nt(qh.astype(BF16), k) * AX_SCALE
            o_ref[:, h * AX_D:(h + 1) * AX_D] = _softmax_pv(s, v).astype(BF16)


def _ax(y2, cache_k, cache_v, gq, gk, tabs, layer, lat, prev=None):
    t, n_seq, nq, kv0, q0 = _geom(lat)
    gp = 1 if lat else AX_KV // 2
    qw = AX_HEADS // AX_KV * AX_D * gp
    kw = AX_D * gp
    in_specs = [
        pl.BlockSpec((TQ, qw), lambda b, i, g: (q0 + b * nq + i, OFF_AQ // qw + g)),
        pl.BlockSpec((t, kw), lambda b, i, g: (kv0 + b, OFF_AK // kw + g)),
        pl.BlockSpec((t, kw), lambda b, i, g: (kv0 + b, OFF_AV // kw + g)),
    ]
    args = [y2, y2, y2]
    if lat:
        in_specs += [pl.BlockSpec((None, None, PAST, kw), lambda b, i, g: (b, layer, 0, g))] * 2
        args += [cache_k, cache_v]
    in_specs += [pl.BlockSpec((None, 1, AX_D), lambda b, i, g: (layer, 0, 0))] * 2
    args += [gq, gk]
    if lat:
        in_specs += [pl.BlockSpec((TQ, LANES), lambda b, i, g: (i, 0))] * 2
        in_specs += [pl.BlockSpec((t, LANES), lambda b, i, g: (0, 0))] * 2
        args += [tabs["h128_cos"], tabs["h128_sin"]] * 2
    aliases = _branch_out(prev, in_specs, args)
    out_specs = [pl.BlockSpec((TQ, qw), lambda b, i, g: (q0 + b * nq + i, g))]
    out_shape = [jax.ShapeDtypeStruct((M, BRANCH_DIM), BF16)]
    if not lat:
        out_specs.append(pl.BlockSpec((None, t, kw), lambda b, i, g: (b, 0, g)))
        out_shape.append(jax.ShapeDtypeStruct((n_seq, t, AX_KV * AX_D), F32))
    return pl.pallas_call(
        functools.partial(_ax_kernel, rope=lat, cached=lat, emit_cache=not lat),
        grid=(n_seq, nq, AX_KV // gp),
        in_specs=in_specs,
        out_specs=out_specs,
        out_shape=out_shape,
        input_output_aliases=aliases,
        compiler_params=_cp(("parallel", "arbitrary", "arbitrary")),
        name="ax_lat" if lat else "ax_ctx",
    )(*args)


def _merge_kernel(b0, b1, b2, b3, w_ref, g0, g1, g2, g3, o_ref, wbf_ref):
    @pl.when(pl.program_id(1) == 0)
    def _():
        wbf_ref[...] = w_ref[...].astype(BF16)

    acc = _sigmoid(g0[...]) * _dot(b0[...], wbf_ref[0])
    acc = acc + _sigmoid(g1[...]) * _dot(b1[...], wbf_ref[1])
    acc = acc + _sigmoid(g2[...]) * _dot(b2[...], wbf_ref[2])
    acc = acc + _sigmoid(g3[...]) * _dot(b3[...], wbf_ref[3])
    o_ref[...] = acc.astype(BF16)


def _merge(branches, w_branch, y2, layer):
    tm, tn = 512, 512
    gate0 = OFF_GATE // tn
    per = D // tn
    b_spec = pl.BlockSpec((tm, BRANCH_DIM), lambda n, m: (m, 0))
    g_specs = [pl.BlockSpec((tm, tn), functools.partial(lambda n, m, i: (m, gate0 + i * per + n), i=i))
               for i in range(4)]
    return pl.pallas_call(
        _merge_kernel,
        grid=(D // tn, M // tm),
        in_specs=[b_spec] * 4 + [pl.BlockSpec((None, 4, BRANCH_DIM, tn), lambda n, m: (layer, 0, 0, n))] + g_specs,
        out_specs=pl.BlockSpec((tm, tn), lambda n, m: (m, n)),
        out_shape=jax.ShapeDtypeStruct((M, D), BF16),
        scratch_shapes=[pltpu.VMEM((4, BRANCH_DIM, tn), BF16)],
        compiler_params=_cp(("parallel", "arbitrary")),
        name="merge",
    )(*branches, w_branch, y2, y2, y2, y2)


def _layer_norm(y, g, b):
    mu = jnp.mean(y, axis=-1, keepdims=True)
    yc = y - mu
    var = jnp.mean(yc * yc, axis=-1, keepdims=True)
    return yc * lax.rsqrt(var + EPS) * g + b


def _ln1_kernel(x_ref, mix_ref, gate_ref, sh_ref, sc_ref, g_ref, b_ref, wr_ref, br_ref,
                x1_ref, h2_ref, logit_ref):
    y = ALPHA * x_ref[...] + gate_ref[...] * mix_ref[...]
    x1 = _layer_norm(y, g_ref[...], b_ref[...])
    x1_ref[...] = x1
    h2 = x1 * (1.0 + sc_ref[...]) + sh_ref[...]
    h2_ref[...] = h2
    logit_ref[...] = jnp.dot(h2, wr_ref[...], precision=HIGHEST, preferred_element_type=F32) + br_ref[...]


def _ln1(x, mix, mods3, ln_g, ln_b, w_route, b_route, layer):
    tm = 256
    row = pl.BlockSpec((tm, D), lambda i: (i, 0))

    def mod(col):
        return pl.BlockSpec((None, 1, D), lambda i: (_mod_row(i, tm), 0, col))

    vec = pl.BlockSpec((None, 1, D), lambda i: (layer, 0, 0))
    return pl.pallas_call(
        _ln1_kernel,
        grid=(M // tm,),
        in_specs=[row, row, mod(2), mod(3), mod(4), vec, vec,
                  pl.BlockSpec((None, D, LANES), lambda i: (layer, 0, 0)),
                  pl.BlockSpec((None, 1, LANES), lambda i: (layer, 0, 0))],
        out_specs=[row, row, pl.BlockSpec((tm, LANES), lambda i: (i, 0))],
        out_shape=[jax.ShapeDtypeStruct((M, D), F32), jax.ShapeDtypeStruct((M, D), F32),
                   jax.ShapeDtypeStruct((M, LANES), F32)],
        compiler_params=_cp(("parallel",)),
        name="ln1",
    )(x, mix, mods3, mods3, mods3, ln_g, ln_b, w_route, b_route)


def _ln2_kernel(dest_ref, x_ref, y_hbm, wk_ref, gate_ref, g_ref, b_ref, sh_ref, sc_ref,
                x2_ref, hb_ref, zbuf, sem):
    tm = x_ref.shape[0]
    base = pl.program_id(0) * tm * TOP_K

    def issue(r, carry):
        for k in range(TOP_K):
            src = dest_ref[base + r * TOP_K + k]
            pltpu.make_async_copy(y_hbm.at[pl.ds(src, 1)], zbuf.at[pl.ds(k * tm + r, 1)], sem).start()
        return carry

    lax.fori_loop(0, tm, issue, 0, unroll=DMA_UNROLL)
    pltpu.make_async_copy(y_hbm.at[pl.ds(0, TOP_K * tm)], zbuf, sem).wait()
    wk = wk_ref[...]
    ff = wk[:, 0:1] * zbuf[0:tm, :] + wk[:, 1:2] * zbuf[tm:2 * tm, :]
    y = ALPHA * x_ref[...] + gate_ref[...] * ff
    x2 = _layer_norm(y, g_ref[...], b_ref[...])
    x2_ref[...] = x2
    hb_ref[...] = (x2 * (1.0 + sc_ref[...]) + sh_ref[...]).astype(BF16)


def _ln2(x1, y_rows, dest, gate_w, mods3, mods3_next, ln_g, ln_b, layer):
    tm = 256
    row = pl.BlockSpec((tm, D), lambda i, d: (i, 0))
    vec = pl.BlockSpec((None, 1, D), lambda i, d: (layer, 0, 0))
    grid_spec = pltpu.PrefetchScalarGridSpec(
        num_scalar_prefetch=1,
        grid=(M // tm,),
        in_specs=[row, pl.BlockSpec(memory_space=pl.ANY),
                  pl.BlockSpec((tm, LANES), lambda i, d: (i, 0)),
                  pl.BlockSpec((None, 1, D), lambda i, d: (_mod_row(i, tm), 0, 5)), vec, vec,
                  pl.BlockSpec((None, 1, D), lambda i, d: (_mod_row(i, tm), 0, 0)),
                  pl.BlockSpec((None, 1, D), lambda i, d: (_mod_row(i, tm), 0, 1))],
        out_specs=[row, row],
        scratch_shapes=[pltpu.VMEM((TOP_K * tm, D), F32), pltpu.SemaphoreType.DMA(())],
    )
    return pl.pallas_call(
        _ln2_kernel,
        grid_spec=grid_spec,
        out_shape=[jax.ShapeDtypeStruct((M, D), F32), jax.ShapeDtypeStruct((M, D), BF16)],
        compiler_params=_cp(("arbitrary",)),
        name="ln2",
    )(dest, x1, y_rows, gate_w, mods3, ln_g, ln_b, mods3_next, mods3_next)


def _expert_changed(be_ref):
    b = pl.program_id(0)
    prev = be_ref[jnp.maximum(b - 1, 0)]
    return (b == 0) | (be_ref[b] != prev)


def _moe_up_kernel(be_ref, off_ref, nu_ref, tok_ref, h_hbm, w1_ref, w3_ref, o_ref, xbuf, w1_bf, w3_bf, sems):
    bm = xbuf.shape[1]
    b = pl.program_id(0)
    n_used = nu_ref[0]

    def gather(blk):
        slot = blk % 2
        base = blk * bm - off_ref[blk]

        def issue(r, carry):
            pltpu.make_async_copy(h_hbm.at[pl.ds(tok_ref[base + r], 1)], xbuf.at[slot, pl.ds(r, 1)],
                                  sems.at[slot]).start()
            return carry

        lax.fori_loop(0, bm, issue, 0, unroll=DMA_UNROLL)

    @pl.when(b == 0)
    def _():
        gather(b)

    @pl.when(b + 1 < n_used)
    def _():
        gather(b + 1)

    @pl.when(b < n_used)
    def _():
        @pl.when(_expert_changed(be_ref))
        def _():
            w1_bf[...] = w1_ref[...].astype(BF16)
            w3_bf[...] = w3_ref[...].astype(BF16)

        slot = b % 2
        pltpu.make_async_copy(h_hbm.at[pl.ds(0, bm)], xbuf.at[slot], sems.at[slot]).wait()
        x = xbuf[slot].astype(BF16)
        a = _dot(x, w1_bf[...])
        c = _dot(x, w3_bf[...])
        o_ref[...] = (a * _sigmoid(a) * c).astype(BF16)

    @pl.when(b >= n_used)
    def _():
        o_ref[...] = jnp.zeros_like(o_ref)


def _moe_up(route, h2, w1, w3, layer):
    w_spec = pl.BlockSpec((None, None, D, D_EXPERT), lambda b, be, off, nu, tok: (layer, be[b], 0, 0))
    grid_spec = pltpu.PrefetchScalarGridSpec(
        num_scalar_prefetch=4,
        grid=(MOE_BLOCKS,),
        in_specs=[pl.BlockSpec(memory_space=pl.ANY), w_spec, w_spec],
        out_specs=pl.BlockSpec((MOE_BM, D_EXPERT), lambda b, be, off, nu, tok: (b, 0)),
        scratch_shapes=[pltpu.VMEM((2, MOE_BM, D), F32), pltpu.VMEM((D, D_EXPERT), BF16),
                        pltpu.VMEM((D, D_EXPERT), BF16), pltpu.SemaphoreType.DMA((2,))],
    )
    return pl.pallas_call(
        _moe_up_kernel,
        grid_spec=grid_spec,
        out_shape=jax.ShapeDtypeStruct((MOE_ROWS, D_EXPERT), BF16),
        compiler_params=_cp(("arbitrary",)),
        name="moe_up",
    )(route["block_e"], route["block_off"], route["n_used"], route["tok_sorted"], h2, w1, w3)


def _moe_down_kernel(be_ref, nu_ref, h_ref, w2_ref, o_ref, w2_bf):
    @pl.when(pl.program_id(0) < nu_ref[0])
    def _():
        @pl.when(_expert_changed(be_ref))
        def _():
            w2_bf[...] = w2_ref[...].astype(BF16)

        o_ref[...] = _dot(h_ref[...], w2_bf[...])

    @pl.when(pl.program_id(0) >= nu_ref[0])
    def _():
        o_ref[...] = jnp.zeros_like(o_ref)


def _moe_down(route, hidden, w2, layer):
    grid_spec = pltpu.PrefetchScalarGridSpec(
        num_scalar_prefetch=2,
        grid=(MOE_BLOCKS,),
        in_specs=[pl.BlockSpec((MOE_BM, D_EXPERT), lambda b, be, nu: (b, 0)),
                  pl.BlockSpec((None, None, D_EXPERT, D), lambda b, be, nu: (layer, be[b], 0, 0))],
        out_specs=pl.BlockSpec((MOE_BM, D), lambda b, be, nu: (b, 0)),
        scratch_shapes=[pltpu.VMEM((D_EXPERT, D), BF16)],
    )
    return pl.pallas_call(
        _moe_down_kernel,
        grid_spec=grid_spec,
        out_shape=jax.ShapeDtypeStruct((MOE_ROWS, D), F32),
        compiler_params=_cp(("arbitrary",)),
        name="moe_down",
    )(route["block_e"], route["n_used"], hidden, w2)


def _route(logits):
    g_logits = logits[:, :N_GROUPS]
    e_logits = logits[:, N_GROUPS:N_GROUPS + N_EXPERTS].reshape(M, N_GROUPS, EPG)
    g_sel = jnp.argmax(g_logits, axis=-1).astype(I32)
    g_w = jnp.take_along_axis(jax.nn.softmax(g_logits, -1), g_sel[:, None], -1)
    e_sel = jnp.take_along_axis(e_logits, g_sel[:, None, None], axis=1)[:, 0]
    top_v, top_i = lax.top_k(e_sel, TOP_K)
    gate = g_w * jax.nn.softmax(top_v, -1)
    expert = g_sel[:, None] * EPG + top_i.astype(I32)
    e_flat = expert.reshape(N_ASSIGN)
    ids = jnp.arange(N_ASSIGN, dtype=I32)
    experts = jnp.arange(N_EXPERTS, dtype=I32)
    e_sorted, order = lax.sort_key_val(e_flat, ids)
    counts = jnp.sum((e_flat[:, None] == experts[None, :]).astype(I32), axis=0)
    padded = (counts + MOE_BM - 1) // MOE_BM * MOE_BM
    start = jnp.cumsum(counts) - counts
    pend = jnp.cumsum(padded)
    shift = pend - padded - start
    dest_sorted = ids + jnp.sum(jnp.where(e_sorted[:, None] == experts[None, :], shift[None, :], 0), axis=1)
    _, dest = lax.sort_key_val(order, dest_sorted)
    block_row = jnp.arange(MOE_BLOCKS, dtype=I32) * MOE_BM
    block_e = jnp.minimum(jnp.sum((pend[None, :] <= block_row[:, None]).astype(I32), axis=1), N_EXPERTS - 1)
    block_off = jnp.sum(jnp.where(block_e[:, None] == experts[None, :], shift[None, :], 0), axis=1)
    return {
        "block_e": block_e.astype(I32),
        "block_off": block_off.astype(I32),
        "n_used": (pend[-1:] // MOE_BM).astype(I32),
        "tok_sorted": jnp.pad(order // TOP_K, (0, MOE_BM)),
        "dest": dest,
        "gate": jnp.pad(gate, ((0, 0), (0, LANES - TOP_K))),
    }


def _rope_tables():
    rows = T_LAT // GRID_W
    row = jnp.repeat(jnp.arange(rows, dtype=I32), GRID_W).astype(F32)
    col = jnp.tile(jnp.arange(GRID_W, dtype=I32), rows).astype(F32)

    def quarter_tables(head_dim):
        m = head_dim // 2
        inv = ROPE_THETA ** (-jnp.arange(0, m, 2, dtype=F32) / m)
        ar = row[:, None] * inv[None, :]
        ac = col[:, None] * inv[None, :]
        cos = jnp.concatenate([jnp.cos(ar), jnp.cos(ar), jnp.cos(ac), jnp.cos(ac)], -1)
        sin = jnp.concatenate([-jnp.sin(ar), jnp.sin(ar), -jnp.sin(ac), jnp.sin(ac)], -1)
        return cos, sin

    c64, s64 = quarter_tables(64)
    c128, s128 = quarter_tables(128)
    return {
        "h64_cos": jnp.concatenate([c64, c64], -1), "h64_sin": jnp.concatenate([s64, s64], -1),
        "h128_cos": c128, "h128_sin": s128,
        "mla_cos": jnp.concatenate([c64, jnp.ones_like(c64)], -1),
        "mla_sin": jnp.concatenate([s64, jnp.zeros_like(s64)], -1),
    }


def _replication_matrix():
    r = np.zeros((SWA_KV * SWA_D, SWA_HEADS * SWA_D), np.float32)
    for g in range(SWA_KV):
        for j in range(SWA_HEADS // SWA_KV):
            for c in range(SWA_D):
                r[g * SWA_D + c, g * 256 + j * SWA_D + c] = 1.0
    return jnp.asarray(r, BF16)


def kernel(x_prompt, x_sample, cache_mla_ckv, cache_mla_kpe, cache_swa_k, cache_swa_v, cache_ax_k, cache_ax_v,
           c, c_ctx, w_in, mla_q_norm, mla_kv_norm, mla_w_q_up, mla_w_kv_up, swa_sink, conv_w, conv_b,
           ax_q_norm, ax_k_norm, w_branch, w_out, w_mod, b_mod, ln1_g, ln1_b, ln2_g, ln2_b,
           moe_w_group, moe_b_group, moe_w_router, moe_b_router, moe_w1, moe_w3, moe_w2):
    x = jnp.concatenate([x_prompt.reshape(M_CTX, D), x_sample.reshape(M_LAT, D)], axis=0)
    cc = jnp.concatenate([c_ctx[None, :], c, jnp.zeros((8 - 1 - N_LAT_SEQ, D), F32)], axis=0)
    mods = _mods(cc, w_mod, b_mod)
    mods3 = [mods[l].reshape(8, 1, 6 * D) for l in range(DEPTH)]

    tabs = _rope_tables()
    rep = _replication_matrix()
    w_in_t = jnp.swapaxes(w_in, 1, 2)
    wq = jnp.pad(mla_w_q_up.reshape(DEPTH, MLA_Q_LORA, MLA_HEADS, MLA_NOPE + MLA_ROPE),
                 ((0, 0), (0, 0), (0, 0), (0, 2 * LANES - MLA_NOPE - MLA_ROPE))
                 ).reshape(DEPTH, MLA_Q_LORA, MLA_HEADS * 2 * LANES).astype(BF16)
    wkv = mla_w_kv_up.astype(BF16)
    cache_kpe = jnp.pad(cache_mla_kpe, ((0, 0), (0, 0), (0, 0), (0, LANES - MLA_ROPE)))
    c_sk = cache_swa_k.reshape(N_LAT_SEQ, DEPTH, PAST, SWA_KV * SWA_D)
    c_sv = cache_swa_v.reshape(N_LAT_SEQ, DEPTH, PAST, SWA_KV * SWA_D)
    c_ak = cache_ax_k.reshape(N_LAT_SEQ, DEPTH, PAST, AX_KV * AX_D)
    c_av = cache_ax_v.reshape(N_LAT_SEQ, DEPTH, PAST, AX_KV * AX_D)
    gq3 = mla_q_norm.reshape(DEPTH, 1, MLA_Q_LORA)
    gkv3 = mla_kv_norm.reshape(DEPTH, 1, MLA_KV_LORA)
    axq3 = ax_q_norm.reshape(DEPTH, 1, AX_D)
    axk3 = ax_k_norm.reshape(DEPTH, 1, AX_D)
    conv_b3 = conv_b.reshape(DEPTH, 1, BRANCH_DIM)
    ln1g, ln1b = ln1_g.reshape(DEPTH, 1, D), ln1_b.reshape(DEPTH, 1, D)
    ln2g, ln2b = ln2_g.reshape(DEPTH, 1, D), ln2_b.reshape(DEPTH, 1, D)
    n_route = N_GROUPS + N_EXPERTS
    w_route = jnp.pad(jnp.concatenate([moe_w_group, moe_w_router], -1), ((0, 0), (0, 0), (0, LANES - n_route)))
    b_route = jnp.pad(jnp.concatenate([moe_b_group, moe_b_router], -1), ((0, 0), (0, LANES - n_route))
                      ).reshape(DEPTH, 1, LANES)

    hb = _modulate(x, mods3[0])
    st = {k: [] for k in ("ckv", "kpe", "sk", "sv", "ak", "av")}
    for l in range(DEPTH):
        y1 = _mm_t(hb, w_in_t, l, 0, 1536, 512, 1024, F32, "w_in_a")
        ykpe = _mm_t(hb, w_in_t, l, 1536, LANES, LANES, 1024, F32, "w_in_kpe")
        y2 = _mm_t(hb, w_in_t, l, 1536, Y2_COLS, 512, 1024, F32, "w_in_b", shift=True)

        blank = jnp.zeros((M, BRANCH_DIM), BF16)
        oa, ckv_c, kpe_c = _mla(y1, ykpe, None, None, wq, wkv, gq3, gkv3, tabs, l, lat=False, prev=blank)
        (oa,) = _mla(y1, ykpe, cache_mla_ckv, cache_kpe, wq, wkv, gq3, gkv3, tabs, l, lat=True, prev=oa)
        ob = _swa(y2, None, None, swa_sink[l], rep, tabs, l, lat=False, prev=blank)
        ob = _swa(y2, c_sk, c_sv, swa_sink[l], rep, tabs, l, lat=True, prev=ob)
        oc = _conv(y2, conv_w, conv_b3, l, lat=False, prev=blank)
        oc = _conv(y2, conv_w, conv_b3, l, lat=True, prev=oc)
        od, ak_c = _ax(y2, None, None, axq3, axk3, tabs, l, lat=False, prev=blank)
        (od,) = _ax(y2, c_ak, c_av, axq3, axk3, tabs, l, lat=True, prev=od)

        st["ckv"].append(ckv_c)
        st["kpe"].append(kpe_c)
        st["sk"].append(y2[:M_CTX, OFF_SK:OFF_SK + 256].reshape(N_CTX_SEQ, T_CTX, SWA_KV, SWA_D))
        st["sv"].append(y2[:M_CTX, OFF_SV:OFF_SV + 256].reshape(N_CTX_SEQ, T_CTX, SWA_KV, SWA_D))
        st["ak"].append(ak_c.reshape(N_CTX_SEQ, T_CTX, AX_KV, AX_D))
        st["av"].append(y2[:M_CTX, OFF_AV:OFF_AV + 512].reshape(N_CTX_SEQ, T_CTX, AX_KV, AX_D))

        merged = _merge([oa, ob, oc, od], w_branch, y2, l)
        mix = _mm(merged, w_out, l, D, 512, 1024, F32, "w_out")
        x1, h2, logits = _ln1(x, mix, mods3[l], ln1g, ln1b, w_route, b_route, l)

        route = _route(logits)
        hidden = _moe_up(route, h2, moe_w1, moe_w3, l)
        y_rows = _moe_down(route, hidden, moe_w2, l)
        x, hb = _ln2(x1, y_rows, route["dest"], route["gate"], mods3[l], mods3[min(l + 1, DEPTH - 1)],
                     ln2g, ln2b, l)

    y_p = x[:M_CTX].reshape(N_CTX_SEQ, T_CTX, D)
    y_s = x[M_CTX:].reshape(N_LAT_SEQ, T_LAT, D)
    return (y_p, y_s,
            jnp.stack(st["ckv"], axis=1), jnp.stack(st["kpe"], axis=1),
            jnp.stack(st["sk"], axis=1), jnp.stack(st["sv"], axis=1),
            jnp.stack(st["ak"], axis=1), jnp.stack(st["av"], axis=1))
```

```python
import functools

import jax
import jax.numpy as jnp
import numpy as np
from jax import lax
from jax.experimental import pallas as pl
from jax.experimental.pallas import tpu as pltpu

F32 = jnp.float32
BF16 = jnp.bfloat16
I32 = jnp.int32
HIGHEST = lax.Precision.HIGHEST

D = 4096
DEPTH = 4
N_CTX_SEQ, T_CTX = 32, 256
N_LAT_SEQ, T_LAT = 2, 1024
PAST = 512
M_CTX = N_CTX_SEQ * T_CTX
M_LAT = N_LAT_SEQ * T_LAT
M = M_CTX + M_LAT
GRID_W = 64
ROPE_THETA = 10000.0
EPS = 1e-6
NEG_INF = -1e30

MLA_HEADS, MLA_NOPE, MLA_ROPE, MLA_V = 8, 128, 64, 128
MLA_Q_LORA, MLA_KV_LORA = 1024, 512
SWA_HEADS, SWA_KV, SWA_D, SWA_WINDOW = 16, 4, 64, 128
AX_HEADS, AX_KV, AX_D = 8, 4, 128
BRANCH_DIM = 1024
N_GROUPS, EPG, N_EXPERTS, TOP_K, D_EXPERT = 8, 8, 64, 2, 512
ALPHA = (2 * DEPTH) ** 0.25
MLA_SCALE = (MLA_NOPE + MLA_ROPE) ** -0.5
SWA_SCALE = SWA_D ** -0.5
AX_SCALE = AX_D ** -0.5

IN_DIM = 24640
Y2_START = 1600
Y2_COLS = IN_DIM - Y2_START
OFF_SQ, OFF_SK, OFF_SV = 0, 1024, 1280
OFF_CB, OFF_CC, OFF_CX = 1536, 2560, 3584
OFF_AQ, OFF_AK, OFF_AV = 4608, 5632, 6144
OFF_GATE = 6656

LANES = 128
VMEM_LIMIT = 56 * 1024 * 1024
W_IN_B_TN = 512

TQ = 256
N_ASSIGN = M * TOP_K
MOE_BM = 256
MOE_BLOCKS = N_ASSIGN // MOE_BM + N_EXPERTS
MOE_ROWS = MOE_BLOCKS * MOE_BM
DMA_UNROLL = 8


def _cp(sem, vmem=VMEM_LIMIT):
    return pltpu.CompilerParams(dimension_semantics=sem, vmem_limit_bytes=vmem)


def _mod_row(i, tm):
    n_ctx = M_CTX // tm
    per_lat = T_LAT // tm
    return jnp.where(i < n_ctx, 0, 1 + (i - n_ctx) // per_lat)


def _sigmoid(x):
    return 0.5 * jnp.tanh(0.5 * x) + 0.5


def _dot(a, b):
    return jnp.dot(a, b, preferred_element_type=F32)


def _dot_nt(a, b):
    return lax.dot_general(a, b, (((1,), (1,)), ((), ())), preferred_element_type=F32)


def _mods_kernel(c_ref, w_ref, b_ref, o_ref):
    c = c_ref[...]
    s = c * _sigmoid(c)
    o_ref[...] = jnp.dot(s, w_ref[...], precision=HIGHEST, preferred_element_type=F32) + b_ref[...]


def _mods(cc, w_mod, b_mod):
    tn = 1024
    n = 6 * D
    return pl.pallas_call(
        _mods_kernel,
        grid=(DEPTH, n // tn),
        in_specs=[
            pl.BlockSpec((8, D), lambda l, j: (0, 0)),
            pl.BlockSpec((None, D, tn), lambda l, j: (l, 0, j)),
            pl.BlockSpec((None, 1, tn), lambda l, j: (l, 0, j)),
        ],
        out_specs=pl.BlockSpec((None, 8, tn), lambda l, j: (l, 0, j)),
        out_shape=jax.ShapeDtypeStruct((DEPTH, 8, n), F32),
        compiler_params=_cp(("parallel", "parallel")),
        name="mods",
    )(cc, w_mod, b_mod.reshape(DEPTH, 1, n))


def _modulate_kernel(x_ref, sh_ref, sc_ref, o_ref):
    o_ref[...] = (x_ref[...] * (1.0 + sc_ref[...]) + sh_ref[...]).astype(BF16)


def _modulate(x, mods3):
    tm = 512
    return pl.pallas_call(
        _modulate_kernel,
        grid=(M // tm,),
        in_specs=[
            pl.BlockSpec((tm, D), lambda i: (i, 0)),
            pl.BlockSpec((None, 1, D), lambda i: (_mod_row(i, tm), 0, 0)),
            pl.BlockSpec((None, 1, D), lambda i: (_mod_row(i, tm), 0, 1)),
        ],
        out_specs=pl.BlockSpec((tm, D), lambda i: (i, 0)),
        out_shape=jax.ShapeDtypeStruct((M, D), BF16),
        compiler_params=_cp(("parallel",)),
        name="modulate",
    )(x, mods3, mods3)


def _mm_kernel(a_ref, w_ref, o_ref, wbf_ref):
    @pl.when(pl.program_id(1) == 0)
    def _():
        wbf_ref[...] = w_ref[...].astype(BF16)

    o_ref[...] = _dot(a_ref[...], wbf_ref[...]).astype(o_ref.dtype)


def _mm(a, w, layer, n_cols, tn, tm, out_dtype, name):
    m_rows, k = a.shape
    return pl.pallas_call(
        _mm_kernel,
        grid=(n_cols // tn, m_rows // tm),
        in_specs=[pl.BlockSpec((tm, k), lambda n, m: (m, 0)),
                  pl.BlockSpec((None, k, tn), lambda n, m: (layer, 0, n))],
        out_specs=pl.BlockSpec((tm, tn), lambda n, m: (m, n)),
        out_shape=jax.ShapeDtypeStruct((m_rows, n_cols), out_dtype),
        scratch_shapes=[pltpu.VMEM((k, tn), BF16)],
        compiler_params=_cp(("parallel", "arbitrary")),
        name=name,
    )(a, w)


def _mm_t_kernel(a_ref, w_ref, o_ref, wbf_ref):
    @pl.when(pl.program_id(1) == 0)
    def _():
        wbf_ref[...] = w_ref[...].astype(BF16)

    o_ref[...] = _dot_nt(a_ref[...], wbf_ref[...]).astype(o_ref.dtype)


def _mm_t_shift_kernel(a_ref, w_ref, wn_ref, o_ref, wbf_ref):
    @pl.when(pl.program_id(1) == 0)
    def _():
        half = LANES // 2
        tn = w_ref.shape[0]
        wbf_ref[0:tn - half, :] = w_ref[half:tn, :].astype(BF16)
        wbf_ref[tn - half:tn, :] = wn_ref[0:half, :].astype(BF16)

    o_ref[...] = _dot_nt(a_ref[...], wbf_ref[...]).astype(o_ref.dtype)


def _mm_t(a, wt, layer, row0, n_cols, tn, tm, out_dtype, name, shift=False, vmem=VMEM_LIMIT):
    m_rows, k = a.shape
    assert n_cols % tn == 0 and row0 % tn == 0 and m_rows % tm == 0
    r0 = row0 // tn
    in_specs = [
        pl.BlockSpec((tm, k), lambda n, m: (m, 0)),
        pl.BlockSpec((None, tn, k), lambda n, m: (layer, r0 + n, 0)),
    ]
    args = [a, wt]
    kern = _mm_t_kernel
    if shift:
        half = LANES // 2
        per = tn // half
        in_specs.append(pl.BlockSpec((None, half, k), lambda n, m: (layer, (r0 + n + 1) * per, 0)))
        args.append(wt)
        kern = _mm_t_shift_kernel
    return pl.pallas_call(
        kern,
        grid=(n_cols // tn, m_rows // tm),
        in_specs=in_specs,
        out_specs=pl.BlockSpec((tm, tn), lambda n, m: (m, n)),
        out_shape=jax.ShapeDtypeStruct((m_rows, n_cols), out_dtype),
        scratch_shapes=[pltpu.VMEM((tn, k), BF16)],
        compiler_params=_cp(("parallel", "arbitrary"), vmem),
        name=name,
    )(*args)


def _rms(x, g):
    return x * lax.rsqrt(jnp.mean(x * x, axis=-1, keepdims=True) + EPS) * g


def _rope_chunk(x, cos, sin, quarter):
    lane = lax.broadcasted_iota(I32, x.shape, 1)
    even = ((lane // quarter) % 2) == 0
    partner = jnp.where(even, pltpu.roll(x, LANES - quarter, 1), pltpu.roll(x, quarter, 1))
    return x * cos + partner * sin


def _rope_wide(x, cos, sin, quarter):
    chunks = [_rope_chunk(x[:, c * LANES:(c + 1) * LANES], cos, sin, quarter) for c in range(x.shape[1] // LANES)]
    return chunks[0] if len(chunks) == 1 else jnp.concatenate(chunks, axis=1)


def _softmax_pv(s, v, sink=None):
    m = jnp.max(s, axis=-1, keepdims=True)
    if sink is not None:
        m = jnp.maximum(m, sink)
    p = jnp.exp(s - m)
    l = jnp.sum(p, axis=-1, keepdims=True)
    if sink is not None:
        l = l + jnp.exp(sink - m)
    return _dot(p.astype(BF16), v) / l


def _geom(lat):
    t = T_LAT if lat else T_CTX
    n_seq = N_LAT_SEQ if lat else N_CTX_SEQ
    kv_row0 = (M_CTX // t) if lat else 0
    q_row0 = (M_CTX // TQ) if lat else 0
    return t, n_seq, t // TQ, kv_row0, q_row0


def _branch_out(prev, in_specs, args):
    aliases = {len(args): 0}
    in_specs.append(pl.BlockSpec(memory_space=pl.ANY))
    args.append(prev)
    return aliases


def _mla_kernel(*refs, rope, cached, emit_cache):
    it = iter(refs)
    qlat_ref, ckv_ref, kpe_ref = next(it), next(it), next(it)
    cckv_ref, ckpe_ref = (next(it), next(it)) if cached else (None, None)
    wq_ref, wkv_ref, gq_ref, gkv_ref = next(it), next(it), next(it), next(it)
    cosq_ref, sinq_ref, cosk_ref, sink_ref = (next(it), next(it), next(it), next(it)) if rope else (None,) * 4
    next(it)
    o_ref = next(it)
    ckv_out, kpe_out = (next(it), next(it)) if emit_cache else (None, None)

    qn = _rms(qlat_ref[...], gq_ref[...]).astype(BF16)
    ckv = _rms(ckv_ref[...], gkv_ref[...])
    kpe = kpe_ref[...]
    if rope:
        kpe = _rope_chunk(kpe, cosk_ref[...], sink_ref[...], MLA_ROPE // 4)
    lane = lax.broadcasted_iota(I32, kpe.shape, 1)
    kpe = jnp.where(lane < MLA_ROPE, kpe, 0.0)
    if emit_cache:
        ckv_out[...] = ckv
        kpe_out[...] = kpe[:, :MLA_ROPE]
    if cached:
        ckv = jnp.concatenate([cckv_ref[...], ckv], axis=0)
        kpe = jnp.concatenate([ckpe_ref[...], kpe], axis=0)
    ckv = ckv.astype(BF16)
    kpe = kpe.astype(BF16)
    hw = 2 * LANES
    for h in range(wq_ref.shape[1] // hw):
        q = _dot(qn, wq_ref[:, h * hw:(h + 1) * hw])
        kv = _dot(ckv, wkv_ref[:, h * hw:(h + 1) * hw]).astype(BF16)
        q_pe = q[:, LANES:]
        if rope:
            q_pe = _rope_chunk(q_pe, cosq_ref[...], sinq_ref[...], MLA_ROPE // 4)
        s = _dot_nt(q[:, :LANES].astype(BF16), kv[:, :LANES])
        s = (s + _dot_nt(q_pe.astype(BF16), kpe)) * MLA_SCALE
        o_ref[:, h * MLA_V:(h + 1) * MLA_V] = _softmax_pv(s, kv[:, LANES:]).astype(BF16)


def _mla(y1, ykpe, cache_ckv, cache_kpe, wq, wkv, gq, gkv, tabs, layer, lat, prev=None):
    t, n_seq, nq, kv0, q0 = _geom(lat)
    hp = 1 if lat else MLA_HEADS
    hw = 2 * LANES * hp
    in_specs = [
        pl.BlockSpec((TQ, MLA_Q_LORA), lambda b, i, h: (q0 + b * nq + i, 0)),
        pl.BlockSpec((t, MLA_KV_LORA), lambda b, i, h: (kv0 + b, MLA_Q_LORA // MLA_KV_LORA)),
        pl.BlockSpec((t, LANES), lambda b, i, h: (kv0 + b, 0)),
    ]
    args = [y1, y1, ykpe]
    if lat:
        in_specs += [
            pl.BlockSpec((None, None, PAST, MLA_KV_LORA), lambda b, i, h: (b, layer, 0, 0)),
            pl.BlockSpec((None, None, PAST, LANES), lambda b, i, h: (b, layer, 0, 0)),
        ]
        args += [cache_ckv, cache_kpe]
    in_specs += [
        pl.BlockSpec((None, MLA_Q_LORA, hw), lambda b, i, h: (layer, 0, h)),
        pl.BlockSpec((None, MLA_KV_LORA, hw), lambda b, i, h: (layer, 0, h)),
        pl.BlockSpec((None, 1, MLA_Q_LORA), lambda b, i, h: (layer, 0, 0)),
        pl.BlockSpec((None, 1, MLA_KV_LORA), lambda b, i, h: (layer, 0, 0)),
    ]
    args += [wq, wkv, gq, gkv]
    if lat:
        in_specs += [pl.BlockSpec((TQ, LANES), lambda b, i, h: (i, 0))] * 2
        in_specs += [pl.BlockSpec((t, LANES), lambda b, i, h: (0, 0))] * 2
        args += [tabs["mla_cos"], tabs["mla_sin"]] * 2
    aliases = _branch_out(prev, in_specs, args)
    out_specs = [pl.BlockSpec((TQ, MLA_V * hp), lambda b, i, h: (q0 + b * nq + i, h))]
    out_shape = [jax.ShapeDtypeStruct((M, BRANCH_DIM), BF16)]
    if not lat:
        out_specs += [
            pl.BlockSpec((None, t, MLA_KV_LORA), lambda b, i, h: (b, 0, 0)),
            pl.BlockSpec((None, t, MLA_ROPE), lambda b, i, h: (b, 0, 0)),
        ]
        out_shape += [
            jax.ShapeDtypeStruct((n_seq, t, MLA_KV_LORA), F32),
            jax.ShapeDtypeStruct((n_seq, t, MLA_ROPE), F32),
        ]
    return pl.pallas_call(
        functools.partial(_mla_kernel, rope=lat, cached=lat, emit_cache=not lat),
        grid=(n_seq, nq, MLA_HEADS // hp),
        in_specs=in_specs,
        out_specs=out_specs,
        out_shape=out_shape,
        input_output_aliases=aliases,
        compiler_params=_cp(("parallel", "arbitrary", "arbitrary")),
        name="mla_lat" if lat else "mla_ctx",
    )(*args)


def _swa_kernel(*refs, rope, cached):
    it = iter(refs)
    sink_ref = next(it)
    q_ref, k_ref, v_ref = next(it), next(it), next(it)
    ck_ref, cv_ref = (next(it), next(it)) if cached else (None, None)
    rep_ref = next(it)
    cosq_ref, sinq_ref, cosk_ref, sink_tab_ref = (next(it), next(it), next(it), next(it)) if rope else (None,) * 4
    next(it)
    o_ref = next(it)

    per_group = SWA_HEADS // SWA_KV
    gw = per_group * SWA_D
    n_groups = q_ref.shape[1] // gw
    g0 = pl.program_id(2) * n_groups
    k = k_ref[...]
    v = v_ref[...]
    if rope:
        k = _rope_wide(k, cosk_ref[...], sink_tab_ref[...], SWA_D // 4)
    if cached:
        k = jnp.concatenate([ck_ref[...], k], axis=0)
        v = jnp.concatenate([cv_ref[...], v], axis=0)
    k = k.astype(BF16)
    v = v.astype(BF16)
    tq = q_ref.shape[0]
    s_len = k.shape[0]
    if cached:
        qpos = pl.program_id(1) * tq + lax.broadcasted_iota(I32, (tq, s_len), 0)
        kpos = lax.broadcasted_iota(I32, (tq, s_len), 1) - PAST
        visible = (kpos < 0) | (jnp.abs(kpos - qpos) <= SWA_WINDOW)
    slot_q = lax.broadcasted_iota(I32, (tq, gw), 1) // SWA_D
    slot_v = lax.broadcasted_iota(I32, (s_len, gw), 1) // SWA_D
    for gi in range(n_groups):
        q = q_ref[:, gi * gw:(gi + 1) * gw]
        if rope:
            q = _rope_wide(q, cosq_ref[...], sinq_ref[...], SWA_D // 4)
        k4 = _dot(k, rep_ref[:, gi * gw:(gi + 1) * gw]).astype(BF16)
        v4 = _dot(v, rep_ref[:, gi * gw:(gi + 1) * gw]).astype(BF16)
        acc = jnp.zeros((tq, gw), F32)
        for j in range(per_group):
            qm = jnp.where(slot_q == j, q, 0.0).astype(BF16)
            s = _dot_nt(qm, k4) * SWA_SCALE
            if cached:
                s = jnp.where(visible, s, NEG_INF)
            vm = jnp.where(slot_v == j, v4, jnp.zeros_like(v4))
            acc = acc + _softmax_pv(s, vm, sink=sink_ref[(g0 + gi) * per_group + j])
        o_ref[:, gi * gw:(gi + 1) * gw] = acc.astype(BF16)


def _swa(y2, cache_k, cache_v, sink, rep, tabs, layer, lat, prev=None):
    t, n_seq, nq, kv0, q0 = _geom(lat)
    gp = 1 if lat else SWA_KV
    gw = SWA_HEADS // SWA_KV * SWA_D * gp
    kvw = SWA_KV * SWA_D
    in_specs = [
        pl.BlockSpec(memory_space=pltpu.SMEM),
        pl.BlockSpec((TQ, gw), lambda b, i, g: (q0 + b * nq + i, OFF_SQ // gw + g)),
        pl.BlockSpec((t, kvw), lambda b, i, g: (kv0 + b, OFF_SK // kvw)),
        pl.BlockSpec((t, kvw), lambda b, i, g: (kv0 + b, OFF_SV // kvw)),
    ]
    args = [sink, y2, y2, y2]
    if lat:
        in_specs += [pl.BlockSpec((None, None, PAST, kvw), lambda b, i, g: (b, layer, 0, 0))] * 2
        args += [cache_k, cache_v]
    in_specs.append(pl.BlockSpec((kvw, gw), lambda b, i, g: (0, g)))
    args.append(rep)
    if lat:
        in_specs += [pl.BlockSpec((TQ, LANES), lambda b, i, g: (i, 0))] * 2
        in_specs += [pl.BlockSpec((t, LANES), lambda b, i, g: (0, 0))] * 2
        args += [tabs["h64_cos"], tabs["h64_sin"]] * 2
    aliases = _branch_out(prev, in_specs, args)
    return pl.pallas_call(
        functools.partial(_swa_kernel, rope=lat, cached=lat),
        grid=(n_seq, nq, SWA_KV // gp),
        in_specs=in_specs,
        out_specs=pl.BlockSpec((TQ, gw), lambda b, i, g: (q0 + b * nq + i, g)),
        out_shape=jax.ShapeDtypeStruct((M, BRANCH_DIM), BF16),
        input_output_aliases=aliases,
        compiler_params=_cp(("parallel", "parallel", "parallel")),
        name="swa_lat" if lat else "swa_ctx",
    )(*args)


def _conv_kernel(cb_ref, cc_ref, cx_ref, w_ref, b_ref, o_ref):
    u = cc_ref[...] * cx_ref[...]
    t = u.shape[0]
    seq_len = jnp.where(pl.program_id(0) < M_CTX // t, T_CTX, T_LAT)
    pos = lax.broadcasted_iota(I32, u.shape, 0) & (seq_len - 1)
    prev = jnp.where(pos == 0, 0.0, pltpu.roll(u, 1, 0))
    nxt = jnp.where(pos == seq_len - 1, 0.0, pltpu.roll(u, t - 1, 0))
    y = w_ref[0:1, :] * prev + w_ref[1:2, :] * u + w_ref[2:3, :] * nxt + b_ref[...]
    o_ref[...] = (cb_ref[...] * y).astype(BF16)


def _conv(y2, conv_w, conv_b, layer):
    t = T_LAT
    cw = 512
    return pl.pallas_call(
        _conv_kernel,
        grid=(M // t, BRANCH_DIM // cw),
        in_specs=[
            pl.BlockSpec((t, cw), lambda b, j: (b, OFF_CB // cw + j)),
            pl.BlockSpec((t, cw), lambda b, j: (b, OFF_CC // cw + j)),
            pl.BlockSpec((t, cw), lambda b, j: (b, OFF_CX // cw + j)),
            pl.BlockSpec((None, 3, cw), lambda b, j: (layer, 0, j)),
            pl.BlockSpec((None, 1, cw), lambda b, j: (layer, 0, j)),
        ],
        out_specs=pl.BlockSpec((t, cw), lambda b, j: (b, j)),
        out_shape=jax.ShapeDtypeStruct((M, BRANCH_DIM), BF16),
        compiler_params=_cp(("parallel", "parallel")),
        name="conv",
    )(y2, y2, y2, conv_w, conv_b)


def _ax_kernel(*refs, rope, cached, emit_cache):
    it = iter(refs)
    q_ref, k_ref, v_ref = next(it), next(it), next(it)
    ck_ref, cv_ref = (next(it), next(it)) if cached else (None, None)
    gq_ref, gk_ref = next(it), next(it)
    cosq_ref, sinq_ref, cosk_ref, sink_ref = (next(it), next(it), next(it), next(it)) if rope else (None,) * 4
    next(it)
    o_ref = next(it)
    k_out = next(it) if emit_cache else None

    per_kv = AX_HEADS // AX_KV
    for g in range(k_ref.shape[1] // AX_D):
        k = _rms(k_ref[:, g * AX_D:(g + 1) * AX_D], gk_ref[...])
        if emit_cache:
            k_out[:, g * AX_D:(g + 1) * AX_D] = k
        if rope:
            k = _rope_chunk(k, cosk_ref[...], sink_ref[...], AX_D // 4)
        v = v_ref[:, g * AX_D:(g + 1) * AX_D]
        if cached:
            k = jnp.concatenate([ck_ref[:, g * AX_D:(g + 1) * AX_D], k], axis=0)
            v = jnp.concatenate([cv_ref[:, g * AX_D:(g + 1) * AX_D], v], axis=0)
        k = k.astype(BF16)
        v = v.astype(BF16)
        for h in range(g * per_kv, (g + 1) * per_kv):
            qh = _rms(q_ref[:, h * AX_D:(h + 1) * AX_D], gq_ref[...])
            if rope:
                qh = _rope_chunk(qh, cosq_ref[...], sinq_ref[...], AX_D // 4)
            s = _dot_nt(qh.astype(BF16), k) * AX_SCALE
            o_ref[:, h * AX_D:(h + 1) * AX_D] = _softmax_pv(s, v).astype(BF16)


def _ax(y2, cache_k, cache_v, gq, gk, tabs, layer, lat, prev=None):
    t, n_seq, nq, kv0, q0 = _geom(lat)
    gp = 1 if lat else AX_KV // 2
    qw = AX_HEADS // AX_KV * AX_D * gp
    kw = AX_D * gp
    in_specs = [
        pl.BlockSpec((TQ, qw), lambda b, i, g: (q0 + b * nq + i, OFF_AQ // qw + g)),
        pl.BlockSpec((t, kw), lambda b, i, g: (kv0 + b, OFF_AK // kw + g)),
        pl.BlockSpec((t, kw), lambda b, i, g: (kv0 + b, OFF_AV // kw + g)),
    ]
    args = [y2, y2, y2]
    if lat:
        in_specs += [pl.BlockSpec((None, None, PAST, kw), lambda b, i, g: (b, layer, 0, g))] * 2
        args += [cache_k, cache_v]
    in_specs += [pl.BlockSpec((None, 1, AX_D), lambda b, i, g: (layer, 0, 0))] * 2
    args += [gq, gk]
    if lat:
        in_specs += [pl.BlockSpec((TQ, LANES), lambda b, i, g: (i, 0))] * 2
        in_specs += [pl.BlockSpec((t, LANES), lambda b, i, g: (0, 0))] * 2
        args += [tabs["h128_cos"], tabs["h128_sin"]] * 2
    aliases = _branch_out(prev, in_specs, args)
    out_specs = [pl.BlockSpec((TQ, qw), lambda b, i, g: (q0 + b * nq + i, g))]
    out_shape = [jax.ShapeDtypeStruct((M, BRANCH_DIM), BF16)]
    if not lat:
        out_specs.append(pl.BlockSpec((None, t, kw), lambda b, i, g: (b, 0, g)))
        out_shape.append(jax.ShapeDtypeStruct((n_seq, t, AX_KV * AX_D), F32))
    return pl.pallas_call(
        functools.partial(_ax_kernel, rope=lat, cached=lat, emit_cache=not lat),
        grid=(n_seq, nq, AX_KV // gp),
        in_specs=in_specs,
        out_specs=out_specs,
        out_shape=out_shape,
        input_output_aliases=aliases,
        compiler_params=_cp(("parallel", "arbitrary", "arbitrary")),
        name="ax_lat" if lat else "ax_ctx",
    )(*args)


def _merge_kernel(b0, b1, b2, b3, w_ref, g0, g1, g2, g3, o_ref, wbf_ref):
    @pl.when(pl.program_id(1) == 0)
    def _():
        wbf_ref[...] = w_ref[...].astype(BF16)

    acc = _sigmoid(g0[...].astype(F32)) * _dot(b0[...], wbf_ref[0])
    acc = acc + _sigmoid(g1[...].astype(F32)) * _dot(b1[...], wbf_ref[1])
    acc = acc + _sigmoid(g2[...].astype(F32)) * _dot(b2[...], wbf_ref[2])
    acc = acc + _sigmoid(g3[...].astype(F32)) * _dot(b3[...], wbf_ref[3])
    o_ref[...] = acc.astype(BF16)


def _merge(branches, w_branch, gates, layer):
    tm, tn = 512, 512
    per = D // tn
    b_spec = pl.BlockSpec((tm, BRANCH_DIM), lambda n, m: (m, 0))
    g_specs = [pl.BlockSpec((tm, tn), functools.partial(lambda n, m, i: (m, i * per + n), i=i))
               for i in range(4)]
    return pl.pallas_call(
        _merge_kernel,
        grid=(D // tn, M // tm),
        in_specs=[b_spec] * 4 + [pl.BlockSpec((None, 4, BRANCH_DIM, tn), lambda n, m: (layer, 0, 0, n))] + g_specs,
        out_specs=pl.BlockSpec((tm, tn), lambda n, m: (m, n)),
        out_shape=jax.ShapeDtypeStruct((M, D), BF16),
        scratch_shapes=[pltpu.VMEM((4, BRANCH_DIM, tn), BF16)],
        compiler_params=_cp(("parallel", "arbitrary")),
        name="merge",
    )(*branches, w_branch, gates, gates, gates, gates)


def _layer_norm(y, g, b):
    mu = jnp.mean(y, axis=-1, keepdims=True)
    yc = y - mu
    var = jnp.mean(yc * yc, axis=-1, keepdims=True)
    return yc * lax.rsqrt(var + EPS) * g + b


def _ln1_kernel(x_ref, mix_ref, gate_ref, sh_ref, sc_ref, g_ref, b_ref, wr_ref, br_ref,
                x1_ref, h2_ref, logit_ref):
    y = ALPHA * x_ref[...] + gate_ref[...] * mix_ref[...]
    x1 = _layer_norm(y, g_ref[...], b_ref[...])
    x1_ref[...] = x1
    h2 = x1 * (1.0 + sc_ref[...]) + sh_ref[...]
    h2_ref[...] = h2
    logit_ref[...] = jnp.dot(h2, wr_ref[...], precision=HIGHEST, preferred_element_type=F32) + br_ref[...]


def _ln1(x, mix, mods3, ln_g, ln_b, w_route, b_route, layer):
    tm = 256
    row = pl.BlockSpec((tm, D), lambda i: (i, 0))

    def mod(col):
        return pl.BlockSpec((None, 1, D), lambda i: (_mod_row(i, tm), 0, col))

    vec = pl.BlockSpec((None, 1, D), lambda i: (layer, 0, 0))
    return pl.pallas_call(
        _ln1_kernel,
        grid=(M // tm,),
        in_specs=[row, row, mod(2), mod(3), mod(4), vec, vec,
                  pl.BlockSpec((None, D, LANES), lambda i: (layer, 0, 0)),
                  pl.BlockSpec((None, 1, LANES), lambda i: (layer, 0, 0))],
        out_specs=[row, row, pl.BlockSpec((tm, LANES), lambda i: (i, 0))],
        out_shape=[jax.ShapeDtypeStruct((M, D), F32), jax.ShapeDtypeStruct((M, D), F32),
                   jax.ShapeDtypeStruct((M, LANES), F32)],
        compiler_params=_cp(("parallel",)),
        name="ln1",
    )(x, mix, mods3, mods3, mods3, ln_g, ln_b, w_route, b_route)


def _ln2_kernel(dest_ref, x_ref, y_hbm, wk_ref, gate_ref, g_ref, b_ref, sh_ref, sc_ref,
                x2_ref, hb_ref, zbuf, sem):
    tm = x_ref.shape[0]
    i = pl.program_id(0)

    def gather(tile):
        slot = tile % 2
        base = tile * tm * TOP_K

        def issue(r, carry):
            for k in range(TOP_K):
                src = dest_ref[base + r * TOP_K + k]
                pltpu.make_async_copy(y_hbm.at[pl.ds(src, 1)], zbuf.at[slot, pl.ds(k * tm + r, 1)],
                                      sem.at[slot]).start()
            return carry

        lax.fori_loop(0, tm, issue, 0, unroll=DMA_UNROLL)

    @pl.when(i == 0)
    def _():
        gather(i)

    @pl.when(i + 1 < pl.num_programs(0))
    def _():
        gather(i + 1)

    slot = i % 2
    pltpu.make_async_copy(y_hbm.at[pl.ds(0, TOP_K * tm)], zbuf.at[slot], sem.at[slot]).wait()
    wk = wk_ref[...]
    ff = wk[:, 0:1] * zbuf[slot, 0:tm, :] + wk[:, 1:2] * zbuf[slot, tm:2 * tm, :]
    y = ALPHA * x_ref[...] + gate_ref[...] * ff
    x2 = _layer_norm(y, g_ref[...], b_ref[...])
    x2_ref[...] = x2
    hb_ref[...] = (x2 * (1.0 + sc_ref[...]) + sh_ref[...]).astype(BF16)


def _ln2(x1, y_rows, dest, gate_w, mods3, mods3_next, ln_g, ln_b, layer):
    tm = 256
    row = pl.BlockSpec((tm, D), lambda i, d: (i, 0))
    vec = pl.BlockSpec((None, 1, D), lambda i, d: (layer, 0, 0))
    grid_spec = pltpu.PrefetchScalarGridSpec(
        num_scalar_prefetch=1,
        grid=(M // tm,),
        in_specs=[row, pl.BlockSpec(memory_space=pl.ANY),
                  pl.BlockSpec((tm, LANES), lambda i, d: (i, 0)),
                  pl.BlockSpec((None, 1, D), lambda i, d: (_mod_row(i, tm), 0, 5)), vec, vec,
                  pl.BlockSpec((None, 1, D), lambda i, d: (_mod_row(i, tm), 0, 0)),
                  pl.BlockSpec((None, 1, D), lambda i, d: (_mod_row(i, tm), 0, 1))],
        out_specs=[row, row],
        scratch_shapes=[pltpu.VMEM((2, TOP_K * tm, D), F32), pltpu.SemaphoreType.DMA((2,))],
    )
    return pl.pallas_call(
        _ln2_kernel,
        grid_spec=grid_spec,
        out_shape=[jax.ShapeDtypeStruct((M, D), F32), jax.ShapeDtypeStruct((M, D), BF16)],
        compiler_params=_cp(("arbitrary",)),
        name="ln2",
    )(dest, x1, y_rows, gate_w, mods3, ln_g, ln_b, mods3_next, mods3_next)


def _expert_changed(be_ref):
    b = pl.program_id(0)
    prev = be_ref[jnp.maximum(b - 1, 0)]
    return (b == 0) | (be_ref[b] != prev)


def _moe_up_kernel(be_ref, off_ref, nu_ref, tok_ref, h_hbm, w1_ref, w3_ref, o_ref, xbuf, w1_bf, w3_bf, sems):
    bm = xbuf.shape[1]
    b = pl.program_id(0)
    n_used = nu_ref[0]

    def gather(blk):
        slot = blk % 2
        base = blk * bm - off_ref[blk]

        def issue(r, carry):
            pltpu.make_async_copy(h_hbm.at[pl.ds(tok_ref[base + r], 1)], xbuf.at[slot, pl.ds(r, 1)],
                                  sems.at[slot]).start()
            return carry

        lax.fori_loop(0, bm, issue, 0, unroll=DMA_UNROLL)

    @pl.when(b == 0)
    def _():
        gather(b)

    @pl.when(b + 1 < n_used)
    def _():
        gather(b + 1)

    @pl.when(b < n_used)
    def _():
        @pl.when(_expert_changed(be_ref))
        def _():
            w1_bf[...] = w1_ref[...].astype(BF16)
            w3_bf[...] = w3_ref[...].astype(BF16)

        slot = b % 2
        pltpu.make_async_copy(h_hbm.at[pl.ds(0, bm)], xbuf.at[slot], sems.at[slot]).wait()
        x = xbuf[slot].astype(BF16)
        a = _dot(x, w1_bf[...])
        c = _dot(x, w3_bf[...])
        o_ref[...] = (a * _sigmoid(a) * c).astype(BF16)

    @pl.when(b >= n_used)
    def _():
        o_ref[...] = jnp.zeros_like(o_ref)


def _moe_up(route, h2, w1, w3, layer):
    w_spec = pl.BlockSpec((None, None, D, D_EXPERT), lambda b, be, off, nu, tok: (layer, be[b], 0, 0))
    grid_spec = pltpu.PrefetchScalarGridSpec(
        num_scalar_prefetch=4,
        grid=(MOE_BLOCKS,),
        in_specs=[pl.BlockSpec(memory_space=pl.ANY), w_spec, w_spec],
        out_specs=pl.BlockSpec((MOE_BM, D_EXPERT), lambda b, be, off, nu, tok: (b, 0)),
        scratch_shapes=[pltpu.VMEM((2, MOE_BM, D), F32), pltpu.VMEM((D, D_EXPERT), BF16),
                        pltpu.VMEM((D, D_EXPERT), BF16), pltpu.SemaphoreType.DMA((2,))],
    )
    return pl.pallas_call(
        _moe_up_kernel,
        grid_spec=grid_spec,
        out_shape=jax.ShapeDtypeStruct((MOE_ROWS, D_EXPERT), BF16),
        compiler_params=_cp(("arbitrary",)),
        name="moe_up",
    )(route["block_e"], route["block_off"], route["n_used"], route["tok_sorted"], h2, w1, w3)


def _moe_down_kernel(be_ref, nu_ref, h_ref, w2_ref, o_ref, w2_bf):
    @pl.when(pl.program_id(0) < nu_ref[0])
    def _():
        @pl.when(_expert_changed(be_ref))
        def _():
            w2_bf[...] = w2_ref[...].astype(BF16)

        o_ref[...] = _dot(h_ref[...], w2_bf[...])

    @pl.when(pl.program_id(0) >= nu_ref[0])
    def _():
        o_ref[...] = jnp.zeros_like(o_ref)


def _moe_down(route, hidden, w2, layer):
    grid_spec = pltpu.PrefetchScalarGridSpec(
        num_scalar_prefetch=2,
        grid=(MOE_BLOCKS,),
        in_specs=[pl.BlockSpec((MOE_BM, D_EXPERT), lambda b, be, nu: (b, 0)),
                  pl.BlockSpec((None, None, D_EXPERT, D), lambda b, be, nu: (layer, be[b], 0, 0))],
        out_specs=pl.BlockSpec((MOE_BM, D), lambda b, be, nu: (b, 0)),
        scratch_shapes=[pltpu.VMEM((D_EXPERT, D), BF16)],
    )
    return pl.pallas_call(
        _moe_down_kernel,
        grid_spec=grid_spec,
        out_shape=jax.ShapeDtypeStruct((MOE_ROWS, D), F32),
        compiler_params=_cp(("arbitrary",)),
        name="moe_down",
    )(route["block_e"], route["n_used"], hidden, w2)


def _route(logits):
    g_logits = logits[:, :N_GROUPS]
    e_logits = logits[:, N_GROUPS:N_GROUPS + N_EXPERTS].reshape(M, N_GROUPS, EPG)
    g_sel = jnp.argmax(g_logits, axis=-1).astype(I32)
    g_w = jnp.take_along_axis(jax.nn.softmax(g_logits, -1), g_sel[:, None], -1)
    e_sel = jnp.take_along_axis(e_logits, g_sel[:, None, None], axis=1)[:, 0]
    top_v, top_i = lax.top_k(e_sel, TOP_K)
    gate = g_w * jax.nn.softmax(top_v, -1)
    expert = g_sel[:, None] * EPG + top_i.astype(I32)
    e_flat = expert.reshape(N_ASSIGN)
    ids = jnp.arange(N_ASSIGN, dtype=I32)
    experts = jnp.arange(N_EXPERTS, dtype=I32)
    e_sorted, order = lax.sort_key_val(e_flat, ids)
    counts = jnp.sum((e_flat[:, None] == experts[None, :]).astype(I32), axis=0)
    padded = (counts + MOE_BM - 1) // MOE_BM * MOE_BM
    start = jnp.cumsum(counts) - counts
    pend = jnp.cumsum(padded)
    shift = pend - padded - start
    dest_sorted = ids + jnp.sum(jnp.where(e_sorted[:, None] == experts[None, :], shift[None, :], 0), axis=1)
    _, dest = lax.sort_key_val(order, dest_sorted)
    block_row = jnp.arange(MOE_BLOCKS, dtype=I32) * MOE_BM
    block_e = jnp.minimum(jnp.sum((pend[None, :] <= block_row[:, None]).astype(I32), axis=1), N_EXPERTS - 1)
    block_off = jnp.sum(jnp.where(block_e[:, None] == experts[None, :], shift[None, :], 0), axis=1)
    return {
        "block_e": block_e.astype(I32),
        "block_off": block_off.astype(I32),
        "n_used": (pend[-1:] // MOE_BM).astype(I32),
        "tok_sorted": jnp.pad(order // TOP_K, (0, MOE_BM)),
        "dest": dest,
        "gate": jnp.pad(gate, ((0, 0), (0, LANES - TOP_K))),
    }


def _rope_tables():
    rows = T_LAT // GRID_W
    row = jnp.repeat(jnp.arange(rows, dtype=I32), GRID_W).astype(F32)
    col = jnp.tile(jnp.arange(GRID_W, dtype=I32), rows).astype(F32)

    def quarter_tables(head_dim):
        m = head_dim // 2
        inv = ROPE_THETA ** (-jnp.arange(0, m, 2, dtype=F32) / m)
        ar = row[:, None] * inv[None, :]
        ac = col[:, None] * inv[None, :]
        cos = jnp.concatenate([jnp.cos(ar), jnp.cos(ar), jnp.cos(ac), jnp.cos(ac)], -1)
        sin = jnp.concatenate([-jnp.sin(ar), jnp.sin(ar), -jnp.sin(ac), jnp.sin(ac)], -1)
        return cos, sin

    c64, s64 = quarter_tables(64)
    c128, s128 = quarter_tables(128)
    return {
        "h64_cos": jnp.concatenate([c64, c64], -1), "h64_sin": jnp.concatenate([s64, s64], -1),
        "h128_cos": c128, "h128_sin": s128,
        "mla_cos": jnp.concatenate([c64, jnp.ones_like(c64)], -1),
        "mla_sin": jnp.concatenate([s64, jnp.zeros_like(s64)], -1),
    }


def _replication_matrix():
    r = np.zeros((SWA_KV * SWA_D, SWA_HEADS * SWA_D), np.float32)
    for g in range(SWA_KV):
        for j in range(SWA_HEADS // SWA_KV):
            for c in range(SWA_D):
                r[g * SWA_D + c, g * 256 + j * SWA_D + c] = 1.0
    return jnp.asarray(r, BF16)


def kernel(x_prompt, x_sample, cache_mla_ckv, cache_mla_kpe, cache_swa_k, cache_swa_v, cache_ax_k, cache_ax_v,
           c, c_ctx, w_in, mla_q_norm, mla_kv_norm, mla_w_q_up, mla_w_kv_up, swa_sink, conv_w, conv_b,
           ax_q_norm, ax_k_norm, w_branch, w_out, w_mod, b_mod, ln1_g, ln1_b, ln2_g, ln2_b,
           moe_w_group, moe_b_group, moe_w_router, moe_b_router, moe_w1, moe_w3, moe_w2):
    x = jnp.concatenate([x_prompt.reshape(M_CTX, D), x_sample.reshape(M_LAT, D)], axis=0)
    cc = jnp.concatenate([c_ctx[None, :], c, jnp.zeros((8 - 1 - N_LAT_SEQ, D), F32)], axis=0)
    mods = _mods(cc, w_mod, b_mod)
    mods3 = [mods[l].reshape(8, 1, 6 * D) for l in range(DEPTH)]

    tabs = _rope_tables()
    rep = _replication_matrix()
    w_in_t = jnp.swapaxes(w_in, 1, 2)
    wq = jnp.pad(mla_w_q_up.reshape(DEPTH, MLA_Q_LORA, MLA_HEADS, MLA_NOPE + MLA_ROPE),
                 ((0, 0), (0, 0), (0, 0), (0, 2 * LANES - MLA_NOPE - MLA_ROPE))
                 ).reshape(DEPTH, MLA_Q_LORA, MLA_HEADS * 2 * LANES).astype(BF16)
    wkv = mla_w_kv_up.astype(BF16)
    cache_kpe = jnp.pad(cache_mla_kpe, ((0, 0), (0, 0), (0, 0), (0, LANES - MLA_ROPE)))
    c_sk = cache_swa_k.reshape(N_LAT_SEQ, DEPTH, PAST, SWA_KV * SWA_D)
    c_sv = cache_swa_v.reshape(N_LAT_SEQ, DEPTH, PAST, SWA_KV * SWA_D)
    c_ak = cache_ax_k.reshape(N_LAT_SEQ, DEPTH, PAST, AX_KV * AX_D)
    c_av = cache_ax_v.reshape(N_LAT_SEQ, DEPTH, PAST, AX_KV * AX_D)
    gq3 = mla_q_norm.reshape(DEPTH, 1, MLA_Q_LORA)
    gkv3 = mla_kv_norm.reshape(DEPTH, 1, MLA_KV_LORA)
    axq3 = ax_q_norm.reshape(DEPTH, 1, AX_D)
    axk3 = ax_k_norm.reshape(DEPTH, 1, AX_D)
    conv_b3 = conv_b.reshape(DEPTH, 1, BRANCH_DIM)
    ln1g, ln1b = ln1_g.reshape(DEPTH, 1, D), ln1_b.reshape(DEPTH, 1, D)
    ln2g, ln2b = ln2_g.reshape(DEPTH, 1, D), ln2_b.reshape(DEPTH, 1, D)
    n_route = N_GROUPS + N_EXPERTS
    w_route = jnp.pad(jnp.concatenate([moe_w_group, moe_w_router], -1), ((0, 0), (0, 0), (0, LANES - n_route)))
    b_route = jnp.pad(jnp.concatenate([moe_b_group, moe_b_router], -1), ((0, 0), (0, LANES - n_route))
                      ).reshape(DEPTH, 1, LANES)

    hb = _modulate(x, mods3[0])
    st = {k: [] for k in ("ckv", "kpe", "sk", "sv", "ak", "av")}
    for l in range(DEPTH):
        y1 = _mm_t(hb, w_in_t, l, 0, 1536, 512, 1024, F32, "w_in_a")
        ykpe = _mm_t(hb, w_in_t, l, 1536, LANES, LANES, 1024, F32, "w_in_kpe")
        y2 = _mm_t(hb, w_in_t, l, 1536, OFF_GATE, W_IN_B_TN, 1024, F32, "w_in_b", shift=True)
        gates = _mm_t(hb, w_in_t, l, 1536 + OFF_GATE, 4 * D, W_IN_B_TN, 1024, BF16, "w_in_gate", shift=True)

        blank = jnp.zeros((M, BRANCH_DIM), BF16)
        oa, ckv_c, kpe_c = _mla(y1, ykpe, None, None, wq, wkv, gq3, gkv3, tabs, l, lat=False, prev=blank)
        (oa,) = _mla(y1, ykpe, cache_mla_ckv, cache_kpe, wq, wkv, gq3, gkv3, tabs, l, lat=True, prev=oa)
        ob = _swa(y2, None, None, swa_sink[l], rep, tabs, l, lat=False, prev=blank)
        ob = _swa(y2, c_sk, c_sv, swa_sink[l], rep, tabs, l, lat=True, prev=ob)
        oc = _conv(y2, conv_w, conv_b3, l)
        od, ak_c = _ax(y2, None, None, axq3, axk3, tabs, l, lat=False, prev=blank)
        (od,) = _ax(y2, c_ak, c_av, axq3, axk3, tabs, l, lat=True, prev=od)

        st["ckv"].append(ckv_c)
        st["kpe"].append(kpe_c)
        st["sk"].append(y2[:M_CTX, OFF_SK:OFF_SK + 256].reshape(N_CTX_SEQ, T_CTX, SWA_KV, SWA_D))
        st["sv"].append(y2[:M_CTX, OFF_SV:OFF_SV + 256].reshape(N_CTX_SEQ, T_CTX, SWA_KV, SWA_D))
        st["ak"].append(ak_c.reshape(N_CTX_SEQ, T_CTX, AX_KV, AX_D))
        st["av"].append(y2[:M_CTX, OFF_AV:OFF_AV + 512].reshape(N_CTX_SEQ, T_CTX, AX_KV, AX_D))

        merged = _merge([oa, ob, oc, od], w_branch, gates, l)
        mix = _mm(merged, w_out, l, D, 512, 1024, F32, "w_out")
        x1, h2, logits = _ln1(x, mix, mods3[l], ln1g, ln1b, w_route, b_route, l)

        route = _route(logits)
        hidden = _moe_up(route, h2, moe_w1, moe_w3, l)
        y_rows = _moe_down(route, hidden, moe_w2, l)
        x, hb = _ln2(x1, y_rows, route["dest"], route["gate"], mods3[l], mods3[min(l + 1, DEPTH - 1)],
                     ln2g, ln2b, l)

    y_p = x[:M_CTX].reshape(N_CTX_SEQ, T_CTX, D)
    y_s = x[M_CTX:].reshape(N_LAT_SEQ, T_LAT, D)
    return (y_p, y_s,
            jnp.stack(st["ckv"], axis=1), jnp.stack(st["kpe"], axis=1),
            jnp.stack(st["sk"], axis=1), jnp.stack(st["sv"], axis=1),
            jnp.stack(st["ak"], axis=1), jnp.stack(st["av"], axis=1))
```

```python
import functools

import jax
import jax.numpy as jnp
import numpy as np
from jax import lax
from jax.experimental import pallas as pl
from jax.experimental.pallas import tpu as pltpu

F32 = jnp.float32
BF16 = jnp.bfloat16
I32 = jnp.int32
U32 = jnp.uint32

D = 4096
DEPTH = 4
N_CTX_SEQ, T_CTX = 32, 256
N_LAT_SEQ, T_LAT = 2, 1024
PAST = 512
M_CTX = N_CTX_SEQ * T_CTX
M_LAT = N_LAT_SEQ * T_LAT
M = M_CTX + M_LAT
GRID_W = 64
ROPE_THETA = 10000.0
EPS = 1e-6
NEG_INF = -1e30

MLA_HEADS, MLA_NOPE, MLA_ROPE, MLA_V = 8, 128, 64, 128
MLA_Q_LORA, MLA_KV_LORA = 1024, 512
SWA_HEADS, SWA_KV, SWA_D, SWA_WINDOW = 16, 4, 64, 128
AX_HEADS, AX_KV, AX_D = 8, 4, 128
BRANCH_DIM = 1024
N_GROUPS, EPG, N_EXPERTS, TOP_K, D_EXPERT = 8, 8, 64, 2, 512
ALPHA = (2 * DEPTH) ** 0.25
MLA_SCALE = (MLA_NOPE + MLA_ROPE) ** -0.5
SWA_SCALE = SWA_D ** -0.5
AX_SCALE = AX_D ** -0.5

IN_DIM = 24640
Y2_START = 1600
Y2_COLS = IN_DIM - Y2_START
OFF_SQ, OFF_SK, OFF_SV = 0, 1024, 1280
OFF_CB, OFF_CC, OFF_CX = 1536, 2560, 3584
OFF_AQ, OFF_AK, OFF_AV = 4608, 5632, 6144
OFF_GATE = 6656

LANES = 128
VMEM_LIMIT = 56 * 1024 * 1024
W_IN_B_TN = 512

TQ = 256
N_ASSIGN = M * TOP_K
MOE_BM = 256
MOE_BLOCKS = N_ASSIGN // MOE_BM + N_EXPERTS
MOE_ROWS = MOE_BLOCKS * MOE_BM
DMA_UNROLL = 8


def _cp(sem, vmem=VMEM_LIMIT):
    return pltpu.CompilerParams(dimension_semantics=sem, vmem_limit_bytes=vmem)


def _mod_row(i, tm):
    n_ctx = M_CTX // tm
    per_lat = T_LAT // tm
    return jnp.where(i < n_ctx, 0, 1 + (i - n_ctx) // per_lat)


def _sigmoid(x):
    return 0.5 * jnp.tanh(0.5 * x) + 0.5


def _dot(a, b):
    return jnp.dot(a, b, preferred_element_type=F32)


def _dot_nt(a, b):
    return lax.dot_general(a, b, (((1,), (1,)), ((), ())), preferred_element_type=F32)


def _pack_bf16_pairs(y):
    half = y.shape[1] // 2
    u = lax.bitcast_convert_type(y, U32)
    r = u + jnp.uint32(0x7FFF) + ((u >> 16) & jnp.uint32(1))
    return (r[:, half:] & jnp.uint32(0xFFFF0000)) | (r[:, :half] >> 16)


def _unpack_bf16_pairs(p):
    lo = lax.bitcast_convert_type(p << 16, F32)
    hi = lax.bitcast_convert_type(p & jnp.uint32(0xFFFF0000), F32)
    return jnp.concatenate([lo, hi], axis=1)


def _mods_kernel(c_ref, w_ref, b_ref, o_ref):
    c = c_ref[...]
    s = c * _sigmoid(c)
    o_ref[...] = _dot(s.astype(BF16), w_ref[...].astype(BF16)) + b_ref[...]


def _mods(cc, w_mod, b_mod):
    tn = 1024
    n = 6 * D
    return pl.pallas_call(
        _mods_kernel,
        grid=(DEPTH, n // tn),
        in_specs=[
            pl.BlockSpec((8, D), lambda l, j: (0, 0)),
            pl.BlockSpec((None, D, tn), lambda l, j: (l, 0, j)),
            pl.BlockSpec((None, 1, tn), lambda l, j: (l, 0, j)),
        ],
        out_specs=pl.BlockSpec((None, 8, tn), lambda l, j: (l, 0, j)),
        out_shape=jax.ShapeDtypeStruct((DEPTH, 8, n), F32),
        compiler_params=_cp(("parallel", "parallel")),
        name="mods",
    )(cc, w_mod, b_mod.reshape(DEPTH, 1, n))


def _modulate_kernel(x_ref, sh_ref, sc_ref, o_ref):
    o_ref[...] = (x_ref[...] * (1.0 + sc_ref[...]) + sh_ref[...]).astype(BF16)


def _modulate(x, mods3):
    tm = 512
    return pl.pallas_call(
        _modulate_kernel,
        grid=(M // tm,),
        in_specs=[
            pl.BlockSpec((tm, D), lambda i: (i, 0)),
            pl.BlockSpec((None, 1, D), lambda i: (_mod_row(i, tm), 0, 0)),
            pl.BlockSpec((None, 1, D), lambda i: (_mod_row(i, tm), 0, 1)),
        ],
        out_specs=pl.BlockSpec((tm, D), lambda i: (i, 0)),
        out_shape=jax.ShapeDtypeStruct((M, D), BF16),
        compiler_params=_cp(("parallel",)),
        name="modulate",
    )(x, mods3, mods3)


def _mm_kernel(a_ref, w_ref, o_ref, wbf_ref):
    @pl.when(pl.program_id(1) == 0)
    def _():
        wbf_ref[...] = w_ref[...].astype(BF16)

    o_ref[...] = _dot(a_ref[...], wbf_ref[...]).astype(o_ref.dtype)


def _mm(a, w, layer, n_cols, tn, tm, out_dtype, name):
    m_rows, k = a.shape
    return pl.pallas_call(
        _mm_kernel,
        grid=(n_cols // tn, m_rows // tm),
        in_specs=[pl.BlockSpec((tm, k), lambda n, m: (m, 0)),
                  pl.BlockSpec((None, k, tn), lambda n, m: (layer, 0, n))],
        out_specs=pl.BlockSpec((tm, tn), lambda n, m: (m, n)),
        out_shape=jax.ShapeDtypeStruct((m_rows, n_cols), out_dtype),
        scratch_shapes=[pltpu.VMEM((k, tn), BF16)],
        compiler_params=_cp(("parallel", "arbitrary")),
        name=name,
    )(a, w)


def _mm_t_kernel(a_ref, w_ref, o_ref, wbf_ref):
    @pl.when(pl.program_id(1) == 0)
    def _():
        wbf_ref[...] = w_ref[...].astype(BF16)

    o_ref[...] = _dot_nt(a_ref[...], wbf_ref[...]).astype(o_ref.dtype)


def _mm_t_shift_kernel(a_ref, w_ref, wn_ref, o_ref, wbf_ref):
    @pl.when(pl.program_id(1) == 0)
    def _():
        half = LANES // 2
        tn = w_ref.shape[0]
        wbf_ref[0:tn - half, :] = w_ref[half:tn, :].astype(BF16)
        wbf_ref[tn - half:tn, :] = wn_ref[0:half, :].astype(BF16)

    o_ref[...] = _dot_nt(a_ref[...], wbf_ref[...]).astype(o_ref.dtype)


def _mm_t(a, wt, layer, row0, n_cols, tn, tm, out_dtype, name, shift=False, vmem=VMEM_LIMIT):
    m_rows, k = a.shape
    assert n_cols % tn == 0 and row0 % tn == 0 and m_rows % tm == 0
    r0 = row0 // tn
    in_specs = [
        pl.BlockSpec((tm, k), lambda n, m: (m, 0)),
        pl.BlockSpec((None, tn, k), lambda n, m: (layer, r0 + n, 0)),
    ]
    args = [a, wt]
    kern = _mm_t_kernel
    if shift:
        half = LANES // 2
        per = tn // half
        in_specs.append(pl.BlockSpec((None, half, k), lambda n, m: (layer, (r0 + n + 1) * per, 0)))
        args.append(wt)
        kern = _mm_t_shift_kernel
    return pl.pallas_call(
        kern,
        grid=(n_cols // tn, m_rows // tm),
        in_specs=in_specs,
        out_specs=pl.BlockSpec((tm, tn), lambda n, m: (m, n)),
        out_shape=jax.ShapeDtypeStruct((m_rows, n_cols), out_dtype),
        scratch_shapes=[pltpu.VMEM((tn, k), BF16)],
        compiler_params=_cp(("parallel", "arbitrary"), vmem),
        name=name,
    )(*args)


def _rms(x, g):
    return x * lax.rsqrt(jnp.mean(x * x, axis=-1, keepdims=True) + EPS) * g


def _rope_chunk(x, cos, sin, quarter):
    lane = lax.broadcasted_iota(I32, x.shape, 1)
    even = ((lane // quarter) % 2) == 0
    partner = jnp.where(even, pltpu.roll(x, LANES - quarter, 1), pltpu.roll(x, quarter, 1))
    return x * cos + partner * sin


def _rope_wide(x, cos, sin, quarter):
    chunks = [_rope_chunk(x[:, c * LANES:(c + 1) * LANES], cos, sin, quarter) for c in range(x.shape[1] // LANES)]
    return chunks[0] if len(chunks) == 1 else jnp.concatenate(chunks, axis=1)


def _softmax_pv(s, v, sink=None):
    m = jnp.max(s, axis=-1, keepdims=True)
    if sink is not None:
        m = jnp.maximum(m, sink)
    p = jnp.exp(s - m)
    l = jnp.sum(p, axis=-1, keepdims=True)
    if sink is not None:
        l = l + jnp.exp(sink - m)
    return _dot(p.astype(BF16), v) / l


def _geom(lat):
    t = T_LAT if lat else T_CTX
    n_seq = N_LAT_SEQ if lat else N_CTX_SEQ
    kv_row0 = (M_CTX // t) if lat else 0
    q_row0 = (M_CTX // TQ) if lat else 0
    return t, n_seq, t // TQ, kv_row0, q_row0


def _branch_out(prev, in_specs, args):
    aliases = {len(args): 0}
    in_specs.append(pl.BlockSpec(memory_space=pl.ANY))
    args.append(prev)
    return aliases


def _mla_kernel(*refs, rope, cached, emit_cache):
    it = iter(refs)
    qlat_ref, ckv_ref, kpe_ref = next(it), next(it), next(it)
    cckv_ref, ckpe_ref = (next(it), next(it)) if cached else (None, None)
    wq_ref, wkv_ref, gq_ref, gkv_ref = next(it), next(it), next(it), next(it)
    cosq_ref, sinq_ref, cosk_ref, sink_ref = (next(it), next(it), next(it), next(it)) if rope else (None,) * 4
    next(it)
    o_ref = next(it)
    ckv_out, kpe_out = (next(it), next(it)) if emit_cache else (None, None)

    qn = _rms(qlat_ref[...], gq_ref[...]).astype(BF16)
    ckv = _rms(ckv_ref[...], gkv_ref[...])
    kpe = kpe_ref[...]
    if rope:
        kpe = _rope_chunk(kpe, cosk_ref[...], sink_ref[...], MLA_ROPE // 4)
    lane = lax.broadcasted_iota(I32, kpe.shape, 1)
    kpe = jnp.where(lane < MLA_ROPE, kpe, 0.0)
    if emit_cache:
        ckv_out[...] = ckv
        kpe_out[...] = kpe[:, :MLA_ROPE]
    if cached:
        ckv = jnp.concatenate([cckv_ref[...], ckv], axis=0)
        kpe = jnp.concatenate([ckpe_ref[...], kpe], axis=0)
    ckv = ckv.astype(BF16)
    kpe = kpe.astype(BF16)
    hw = 2 * LANES
    for h in range(wq_ref.shape[1] // hw):
        q = _dot(qn, wq_ref[:, h * hw:(h + 1) * hw])
        kv = _dot(ckv, wkv_ref[:, h * hw:(h + 1) * hw]).astype(BF16)
        q_pe = q[:, LANES:]
        if rope:
            q_pe = _rope_chunk(q_pe, cosq_ref[...], sinq_ref[...], MLA_ROPE // 4)
        s = _dot_nt(q[:, :LANES].astype(BF16), kv[:, :LANES])
        s = (s + _dot_nt(q_pe.astype(BF16), kpe)) * MLA_SCALE
        o_ref[:, h * MLA_V:(h + 1) * MLA_V] = _softmax_pv(s, kv[:, LANES:]).astype(BF16)


def _mla(y1, ykpe, cache_ckv, cache_kpe, wq, wkv, gq, gkv, tabs, layer, lat, prev=None):
    t, n_seq, nq, kv0, q0 = _geom(lat)
    hp = 1 if lat else MLA_HEADS
    hw = 2 * LANES * hp
    in_specs = [
        pl.BlockSpec((TQ, MLA_Q_LORA), lambda b, i, h: (q0 + b * nq + i, 0)),
        pl.BlockSpec((t, MLA_KV_LORA), lambda b, i, h: (kv0 + b, MLA_Q_LORA // MLA_KV_LORA)),
        pl.BlockSpec((t, LANES), lambda b, i, h: (kv0 + b, 0)),
    ]
    args = [y1, y1, ykpe]
    if lat:
        in_specs += [
            pl.BlockSpec((None, None, PAST, MLA_KV_LORA), lambda b, i, h: (b, layer, 0, 0)),
            pl.BlockSpec((None, None, PAST, LANES), lambda b, i, h: (b, layer, 0, 0)),
        ]
        args += [cache_ckv, cache_kpe]
    in_specs += [
        pl.BlockSpec((None, MLA_Q_LORA, hw), lambda b, i, h: (layer, 0, h)),
        pl.BlockSpec((None, MLA_KV_LORA, hw), lambda b, i, h: (layer, 0, h)),
        pl.BlockSpec((None, 1, MLA_Q_LORA), lambda b, i, h: (layer, 0, 0)),
        pl.BlockSpec((None, 1, MLA_KV_LORA), lambda b, i, h: (layer, 0, 0)),
    ]
    args += [wq, wkv, gq, gkv]
    if lat:
        in_specs += [pl.BlockSpec((TQ, LANES), lambda b, i, h: (i, 0))] * 2
        in_specs += [pl.BlockSpec((t, LANES), lambda b, i, h: (0, 0))] * 2
        args += [tabs["mla_cos"], tabs["mla_sin"]] * 2
    aliases = _branch_out(prev, in_specs, args)
    out_specs = [pl.BlockSpec((TQ, MLA_V * hp), lambda b, i, h: (q0 + b * nq + i, h))]
    out_shape = [jax.ShapeDtypeStruct((M, BRANCH_DIM), BF16)]
    if not lat:
        out_specs += [
            pl.BlockSpec((None, t, MLA_KV_LORA), lambda b, i, h: (b, 0, 0)),
            pl.BlockSpec((None, t, MLA_ROPE), lambda b, i, h: (b, 0, 0)),
        ]
        out_shape += [
            jax.ShapeDtypeStruct((n_seq, t, MLA_KV_LORA), F32),
            jax.ShapeDtypeStruct((n_seq, t, MLA_ROPE), F32),
        ]
    return pl.pallas_call(
        functools.partial(_mla_kernel, rope=lat, cached=lat, emit_cache=not lat),
        grid=(n_seq, nq, MLA_HEADS // hp),
        in_specs=in_specs,
        out_specs=out_specs,
        out_shape=out_shape,
        input_output_aliases=aliases,
        compiler_params=_cp(("parallel", "arbitrary", "arbitrary")),
        name="mla_lat" if lat else "mla_ctx",
    )(*args)


def _swa_kernel(*refs, rope, cached):
    it = iter(refs)
    sink_ref = next(it)
    q_ref, k_ref, v_ref = next(it), next(it), next(it)
    ck_ref, cv_ref = (next(it), next(it)) if cached else (None, None)
    rep_ref = next(it)
    cosq_ref, sinq_ref, cosk_ref, sink_tab_ref = (next(it), next(it), next(it), next(it)) if rope else (None,) * 4
    next(it)
    o_ref = next(it)

    per_group = SWA_HEADS // SWA_KV
    gw = per_group * SWA_D
    n_groups = q_ref.shape[1] // gw
    g0 = pl.program_id(2) * n_groups
    k = k_ref[...]
    v = v_ref[...]
    if rope:
        k = _rope_wide(k, cosk_ref[...], sink_tab_ref[...], SWA_D // 4)
    if cached:
        k = jnp.concatenate([ck_ref[...], k], axis=0)
        v = jnp.concatenate([cv_ref[...], v], axis=0)
    k = k.astype(BF16)
    v = v.astype(BF16)
    tq = q_ref.shape[0]
    s_len = k.shape[0]
    if cached:
        qpos = pl.program_id(1) * tq + lax.broadcasted_iota(I32, (tq, s_len), 0)
        kpos = lax.broadcasted_iota(I32, (tq, s_len), 1) - PAST
        visible = (kpos < 0) | (jnp.abs(kpos - qpos) <= SWA_WINDOW)
    slot_q = lax.broadcasted_iota(I32, (tq, gw), 1) // SWA_D
    slot_v = lax.broadcasted_iota(I32, (s_len, gw), 1) // SWA_D
    for gi in range(n_groups):
        q = q_ref[:, gi * gw:(gi + 1) * gw]
        if rope:
            q = _rope_wide(q, cosq_ref[...], sinq_ref[...], SWA_D // 4)
        k4 = _dot(k, rep_ref[:, gi * gw:(gi + 1) * gw]).astype(BF16)
        v4 = _dot(v, rep_ref[:, gi * gw:(gi + 1) * gw]).astype(BF16)
        acc = jnp.zeros((tq, gw), F32)
        for j in range(per_group):
            qm = jnp.where(slot_q == j, q, 0.0).astype(BF16)
            s = _dot_nt(qm, k4) * SWA_SCALE
            if cached:
                s = jnp.where(visible, s, NEG_INF)
            vm = jnp.where(slot_v == j, v4, jnp.zeros_like(v4))
            acc = acc + _softmax_pv(s, vm, sink=sink_ref[(g0 + gi) * per_group + j])
        o_ref[:, gi * gw:(gi + 1) * gw] = acc.astype(BF16)


def _swa(y2, cache_k, cache_v, sink, rep, tabs, layer, lat, prev=None):
    t, n_seq, nq, kv0, q0 = _geom(lat)
    gp = 1 if lat else SWA_KV
    gw = SWA_HEADS // SWA_KV * SWA_D * gp
    kvw = SWA_KV * SWA_D
    in_specs = [
        pl.BlockSpec(memory_space=pltpu.SMEM),
        pl.BlockSpec((TQ, gw), lambda b, i, g: (q0 + b * nq + i, OFF_SQ // gw + g)),
        pl.BlockSpec((t, kvw), lambda b, i, g: (kv0 + b, OFF_SK // kvw)),
        pl.BlockSpec((t, kvw), lambda b, i, g: (kv0 + b, OFF_SV // kvw)),
    ]
    args = [sink, y2, y2, y2]
    if lat:
        in_specs += [pl.BlockSpec((None, None, PAST, kvw), lambda b, i, g: (b, layer, 0, 0))] * 2
        args += [cache_k, cache_v]
    in_specs.append(pl.BlockSpec((kvw, gw), lambda b, i, g: (0, g)))
    args.append(rep)
    if lat:
        in_specs += [pl.BlockSpec((TQ, LANES), lambda b, i, g: (i, 0))] * 2
        in_specs += [pl.BlockSpec((t, LANES), lambda b, i, g: (0, 0))] * 2
        args += [tabs["h64_cos"], tabs["h64_sin"]] * 2
    aliases = _branch_out(prev, in_specs, args)
    return pl.pallas_call(
        functools.partial(_swa_kernel, rope=lat, cached=lat),
        grid=(n_seq, nq, SWA_KV // gp),
        in_specs=in_specs,
        out_specs=pl.BlockSpec((TQ, gw), lambda b, i, g: (q0 + b * nq + i, g)),
        out_shape=jax.ShapeDtypeStruct((M, BRANCH_DIM), BF16),
        input_output_aliases=aliases,
        compiler_params=_cp(("parallel", "parallel", "parallel")),
        name="swa_lat" if lat else "swa_ctx",
    )(*args)


def _conv_kernel(cb_ref, cc_ref, cx_ref, w_ref, b_ref, o_ref):
    u = cc_ref[...] * cx_ref[...]
    t = u.shape[0]
    seq_len = jnp.where(pl.program_id(0) < M_CTX // t, T_CTX, T_LAT)
    pos = lax.broadcasted_iota(I32, u.shape, 0) & (seq_len - 1)
    prev = jnp.where(pos == 0, 0.0, pltpu.roll(u, 1, 0))
    nxt = jnp.where(pos == seq_len - 1, 0.0, pltpu.roll(u, t - 1, 0))
    y = w_ref[0:1, :] * prev + w_ref[1:2, :] * u + w_ref[2:3, :] * nxt + b_ref[...]
    o_ref[...] = (cb_ref[...] * y).astype(BF16)


def _conv(y2, conv_w, conv_b, layer):
    t = T_LAT
    cw = 512
    return pl.pallas_call(
        _conv_kernel,
        grid=(M // t, BRANCH_DIM // cw),
        in_specs=[
            pl.BlockSpec((t, cw), lambda b, j: (b, OFF_CB // cw + j)),
            pl.BlockSpec((t, cw), lambda b, j: (b, OFF_CC // cw + j)),
            pl.BlockSpec((t, cw), lambda b, j: (b, OFF_CX // cw + j)),
            pl.BlockSpec((None, 3, cw), lambda b, j: (layer, 0, j)),
            pl.BlockSpec((None, 1, cw), lambda b, j: (layer, 0, j)),
        ],
        out_specs=pl.BlockSpec((t, cw), lambda b, j: (b, j)),
        out_shape=jax.ShapeDtypeStruct((M, BRANCH_DIM), BF16),
        compiler_params=_cp(("parallel", "parallel")),
        name="conv",
    )(y2, y2, y2, conv_w, conv_b)


def _ax_kernel(*refs, rope, cached, emit_cache):
    it = iter(refs)
    q_ref, k_ref, v_ref = next(it), next(it), next(it)
    ck_ref, cv_ref = (next(it), next(it)) if cached else (None, None)
    gq_ref, gk_ref = next(it), next(it)
    cosq_ref, sinq_ref, cosk_ref, sink_ref = (next(it), next(it), next(it), next(it)) if rope else (None,) * 4
    next(it)
    o_ref = next(it)
    k_out = next(it) if emit_cache else None

    per_kv = AX_HEADS // AX_KV
    for g in range(k_ref.shape[1] // AX_D):
        k = _rms(k_ref[:, g * AX_D:(g + 1) * AX_D], gk_ref[...])
        if emit_cache:
            k_out[:, g * AX_D:(g + 1) * AX_D] = k
        if rope:
            k = _rope_chunk(k, cosk_ref[...], sink_ref[...], AX_D // 4)
        v = v_ref[:, g * AX_D:(g + 1) * AX_D]
        if cached:
            k = jnp.concatenate([ck_ref[:, g * AX_D:(g + 1) * AX_D], k], axis=0)
            v = jnp.concatenate([cv_ref[:, g * AX_D:(g + 1) * AX_D], v], axis=0)
        k = k.astype(BF16)
        v = v.astype(BF16)
        for h in range(g * per_kv, (g + 1) * per_kv):
            qh = _rms(q_ref[:, h * AX_D:(h + 1) * AX_D], gq_ref[...])
            if rope:
                qh = _rope_chunk(qh, cosq_ref[...], sinq_ref[...], AX_D // 4)
            s = _dot_nt(qh.astype(BF16), k) * AX_SCALE
            o_ref[:, h * AX_D:(h + 1) * AX_D] = _softmax_pv(s, v).astype(BF16)


def _ax(y2, cache_k, cache_v, gq, gk, tabs, layer, lat, prev=None):
    t, n_seq, nq, kv0, q0 = _geom(lat)
    gp = 1 if lat else AX_KV // 2
    qw = AX_HEADS // AX_KV * AX_D * gp
    kw = AX_D * gp
    in_specs = [
        pl.BlockSpec((TQ, qw), lambda b, i, g: (q0 + b * nq + i, OFF_AQ // qw + g)),
        pl.BlockSpec((t, kw), lambda b, i, g: (kv0 + b, OFF_AK // kw + g)),
        pl.BlockSpec((t, kw), lambda b, i, g: (kv0 + b, OFF_AV // kw + g)),
    ]
    args = [y2, y2, y2]
    if lat:
        in_specs += [pl.BlockSpec((None, None, PAST, kw), lambda b, i, g: (b, layer, 0, g))] * 2
        args += [cache_k, cache_v]
    in_specs += [pl.BlockSpec((None, 1, AX_D), lambda b, i, g: (layer, 0, 0))] * 2
    args += [gq, gk]
    if lat:
        in_specs += [pl.BlockSpec((TQ, LANES), lambda b, i, g: (i, 0))] * 2
        in_specs += [pl.BlockSpec((t, LANES), lambda b, i, g: (0, 0))] * 2
        args += [tabs["h128_cos"], tabs["h128_sin"]] * 2
    aliases = _branch_out(prev, in_specs, args)
    out_specs = [pl.BlockSpec((TQ, qw), lambda b, i, g: (q0 + b * nq + i, g))]
    out_shape = [jax.ShapeDtypeStruct((M, BRANCH_DIM), BF16)]
    if not lat:
        out_specs.append(pl.BlockSpec((None, t, kw), lambda b, i, g: (b, 0, g)))
        out_shape.append(jax.ShapeDtypeStruct((n_seq, t, AX_KV * AX_D), F32))
    return pl.pallas_call(
        functools.partial(_ax_kernel, rope=lat, cached=lat, emit_cache=not lat),
        grid=(n_seq, nq, AX_KV // gp),
        in_specs=in_specs,
        out_specs=out_specs,
        out_shape=out_shape,
        input_output_aliases=aliases,
        compiler_params=_cp(("parallel", "arbitrary", "arbitrary")),
        name="ax_lat" if lat else "ax_ctx",
    )(*args)


def _merge_kernel(b0, b1, b2, b3, w_ref, g0, g1, g2, g3, o_ref, wbf_ref):
    @pl.when(pl.program_id(1) == 0)
    def _():
        wbf_ref[...] = w_ref[...].astype(BF16)

    acc = _sigmoid(g0[...].astype(F32)) * _dot(b0[...], wbf_ref[0])
    acc = acc + _sigmoid(g1[...].astype(F32)) * _dot(b1[...], wbf_ref[1])
    acc = acc + _sigmoid(g2[...].astype(F32)) * _dot(b2[...], wbf_ref[2])
    acc = acc + _sigmoid(g3[...].astype(F32)) * _dot(b3[...], wbf_ref[3])
    o_ref[...] = acc.astype(BF16)


def _merge(branches, w_branch, gates, layer):
    tm, tn = 512, 512
    per = D // tn
    b_spec = pl.BlockSpec((tm, BRANCH_DIM), lambda n, m: (m, 0))
    g_specs = [pl.BlockSpec((tm, tn), functools.partial(lambda n, m, i: (m, i * per + n), i=i))
               for i in range(4)]
    return pl.pallas_call(
        _merge_kernel,
        grid=(D // tn, M // tm),
        in_specs=[b_spec] * 4 + [pl.BlockSpec((None, 4, BRANCH_DIM, tn), lambda n, m: (layer, 0, 0, n))] + g_specs,
        out_specs=pl.BlockSpec((tm, tn), lambda n, m: (m, n)),
        out_shape=jax.ShapeDtypeStruct((M, D), BF16),
        scratch_shapes=[pltpu.VMEM((4, BRANCH_DIM, tn), BF16)],
        compiler_params=_cp(("parallel", "arbitrary")),
        name="merge",
    )(*branches, w_branch, gates, gates, gates, gates)


def _layer_norm(y, g, b):
    mu = jnp.mean(y, axis=-1, keepdims=True)
    yc = y - mu
    var = jnp.mean(yc * yc, axis=-1, keepdims=True)
    return yc * lax.rsqrt(var + EPS) * g + b


def _ln1_kernel(x_ref, mix_ref, gate_ref, sh_ref, sc_ref, g_ref, b_ref, wr_ref, br_ref,
                x1_ref, h2_ref, logit_ref):
    y = ALPHA * x_ref[...] + gate_ref[...] * mix_ref[...].astype(F32)
    x1 = _layer_norm(y, g_ref[...], b_ref[...])
    x1_ref[...] = x1
    h2 = x1 * (1.0 + sc_ref[...]) + sh_ref[...]
    h2_ref[...] = h2
    h_hi = h2.astype(BF16)
    h_lo = (h2 - h_hi.astype(F32)).astype(BF16)
    logit_ref[...] = (_dot(h_hi, wr_ref[0]) + (_dot(h_hi, wr_ref[1]) + _dot(h_lo, wr_ref[0]))) + br_ref[...]


def _ln1(x, mix, mods3, ln_g, ln_b, w_route, b_route, layer):
    tm = 256
    row = pl.BlockSpec((tm, D), lambda i: (i, 0))

    def mod(col):
        return pl.BlockSpec((None, 1, D), lambda i: (_mod_row(i, tm), 0, col))

    vec = pl.BlockSpec((None, 1, D), lambda i: (layer, 0, 0))
    return pl.pallas_call(
        _ln1_kernel,
        grid=(M // tm,),
        in_specs=[row, row, mod(2), mod(3), mod(4), vec, vec,
                  pl.BlockSpec((None, 2, D, LANES), lambda i: (layer, 0, 0, 0)),
                  pl.BlockSpec((None, 1, LANES), lambda i: (layer, 0, 0))],
        out_specs=[row, row, pl.BlockSpec((tm, LANES), lambda i: (i, 0))],
        out_shape=[jax.ShapeDtypeStruct((M, D), F32), jax.ShapeDtypeStruct((M, D), F32),
                   jax.ShapeDtypeStruct((M, LANES), F32)],
        compiler_params=_cp(("parallel",)),
        name="ln1",
    )(x, mix, mods3, mods3, mods3, ln_g, ln_b, w_route, b_route)


def _ln2_kernel(dest_ref, x_ref, y_hbm, wk_ref, gate_ref, g_ref, b_ref, sh_ref, sc_ref,
                x2_ref, hb_ref, zbuf, sem):
    tm = x_ref.shape[0]
    i = pl.program_id(0)

    def gather(tile):
        slot = tile % 2
        base = tile * tm * TOP_K

        def issue(r, carry):
            for k in range(TOP_K):
                src = dest_ref[base + r * TOP_K + k]
                pltpu.make_async_copy(y_hbm.at[pl.ds(src, 1)], zbuf.at[slot, pl.ds(k * tm + r, 1)],
                                      sem.at[slot]).start()
            return carry

        lax.fori_loop(0, tm, issue, 0, unroll=DMA_UNROLL)

    @pl.when(i == 0)
    def _():
        gather(i)

    @pl.when(i + 1 < pl.num_programs(0))
    def _():
        gather(i + 1)

    slot = i % 2
    pltpu.make_async_copy(y_hbm.at[pl.ds(0, TOP_K * tm)], zbuf.at[slot], sem.at[slot]).wait()
    wk = wk_ref[...]
    z0 = _unpack_bf16_pairs(zbuf[slot, 0:tm, :])
    z1 = _unpack_bf16_pairs(zbuf[slot, tm:2 * tm, :])
    ff = wk[:, 0:1] * z0 + wk[:, 1:2] * z1
    y = ALPHA * x_ref[...] + gate_ref[...] * ff
    x2 = _layer_norm(y, g_ref[...], b_ref[...])
    x2_ref[...] = x2
    hb_ref[...] = (x2 * (1.0 + sc_ref[...]) + sh_ref[...]).astype(BF16)


def _ln2(x1, y_rows, dest, gate_w, mods3, mods3_next, ln_g, ln_b, layer):
    tm = 256
    row = pl.BlockSpec((tm, D), lambda i, d: (i, 0))
    vec = pl.BlockSpec((None, 1, D), lambda i, d: (layer, 0, 0))
    grid_spec = pltpu.PrefetchScalarGridSpec(
        num_scalar_prefetch=1,
        grid=(M // tm,),
        in_specs=[row, pl.BlockSpec(memory_space=pl.ANY),
                  pl.BlockSpec((tm, LANES), lambda i, d: (i, 0)),
                  pl.BlockSpec((None, 1, D), lambda i, d: (_mod_row(i, tm), 0, 5)), vec, vec,
                  pl.BlockSpec((None, 1, D), lambda i, d: (_mod_row(i, tm), 0, 0)),
                  pl.BlockSpec((None, 1, D), lambda i, d: (_mod_row(i, tm), 0, 1))],
        out_specs=[row, row],
        scratch_shapes=[pltpu.VMEM((2, TOP_K * tm, D // 2), U32), pltpu.SemaphoreType.DMA((2,))],
    )
    return pl.pallas_call(
        _ln2_kernel,
        grid_spec=grid_spec,
        out_shape=[jax.ShapeDtypeStruct((M, D), F32), jax.ShapeDtypeStruct((M, D), BF16)],
        compiler_params=_cp(("arbitrary",)),
        name="ln2",
    )(dest, x1, y_rows, gate_w, mods3, ln_g, ln_b, mods3_next, mods3_next)


def _expert_changed(be_ref):
    b = pl.program_id(0)
    prev = be_ref[jnp.maximum(b - 1, 0)]
    return (b == 0) | (be_ref[b] != prev)


def _moe_up_kernel(be_ref, off_ref, nu_ref, tok_ref, h_hbm, w1_ref, w3_ref, o_ref, xbuf, w1_bf, w3_bf, sems):
    bm = xbuf.shape[1]
    b = pl.program_id(0)
    n_used = nu_ref[0]

    def gather(blk):
        slot = blk % 2
        base = blk * bm - off_ref[blk]

        def issue(r, carry):
            pltpu.make_async_copy(h_hbm.at[pl.ds(tok_ref[base + r], 1)], xbuf.at[slot, pl.ds(r, 1)],
                                  sems.at[slot]).start()
            return carry

        lax.fori_loop(0, bm, issue, 0, unroll=DMA_UNROLL)

    @pl.when(b == 0)
    def _():
        gather(b)

    @pl.when(b + 1 < n_used)
    def _():
        gather(b + 1)

    @pl.when(b < n_used)
    def _():
        @pl.when(_expert_changed(be_ref))
        def _():
            w1_bf[...] = w1_ref[...].astype(BF16)
            w3_bf[...] = w3_ref[...].astype(BF16)

        slot = b % 2
        pltpu.make_async_copy(h_hbm.at[pl.ds(0, bm)], xbuf.at[slot], sems.at[slot]).wait()
        x = xbuf[slot].astype(BF16)
        a = _dot(x, w1_bf[...])
        c = _dot(x, w3_bf[...])
        o_ref[...] = (a * _sigmoid(a) * c).astype(BF16)

    @pl.when(b >= n_used)
    def _():
        o_ref[...] = jnp.zeros_like(o_ref)


def _moe_up(route, h2, w1, w3, layer):
    w_spec = pl.BlockSpec((None, None, D, D_EXPERT), lambda b, be, off, nu, tok: (layer, be[b], 0, 0))
    grid_spec = pltpu.PrefetchScalarGridSpec(
        num_scalar_prefetch=4,
        grid=(MOE_BLOCKS,),
        in_specs=[pl.BlockSpec(memory_space=pl.ANY), w_spec, w_spec],
        out_specs=pl.BlockSpec((MOE_BM, D_EXPERT), lambda b, be, off, nu, tok: (b, 0)),
        scratch_shapes=[pltpu.VMEM((2, MOE_BM, D), F32), pltpu.VMEM((D, D_EXPERT), BF16),
                        pltpu.VMEM((D, D_EXPERT), BF16), pltpu.SemaphoreType.DMA((2,))],
    )
    return pl.pallas_call(
        _moe_up_kernel,
        grid_spec=grid_spec,
        out_shape=jax.ShapeDtypeStruct((MOE_ROWS, D_EXPERT), BF16),
        compiler_params=_cp(("arbitrary",)),
        name="moe_up",
    )(route["block_e"], route["block_off"], route["n_used"], route["tok_sorted"], h2, w1, w3)


def _moe_down_kernel(be_ref, nu_ref, h_ref, w2_ref, o_ref, w2_bf):
    @pl.when(pl.program_id(0) < nu_ref[0])
    def _():
        @pl.when(_expert_changed(be_ref))
        def _():
            w2_bf[...] = w2_ref[...].astype(BF16)

        o_ref[...] = _pack_bf16_pairs(_dot(h_ref[...], w2_bf[...]))

    @pl.when(pl.program_id(0) >= nu_ref[0])
    def _():
        o_ref[...] = jnp.zeros_like(o_ref)


def _moe_down(route, hidden, w2, layer):
    grid_spec = pltpu.PrefetchScalarGridSpec(
        num_scalar_prefetch=2,
        grid=(MOE_BLOCKS,),
        in_specs=[pl.BlockSpec((MOE_BM, D_EXPERT), lambda b, be, nu: (b, 0)),
                  pl.BlockSpec((None, None, D_EXPERT, D), lambda b, be, nu: (layer, be[b], 0, 0))],
        out_specs=pl.BlockSpec((MOE_BM, D // 2), lambda b, be, nu: (b, 0)),
        scratch_shapes=[pltpu.VMEM((D_EXPERT, D), BF16)],
    )
    return pl.pallas_call(
        _moe_down_kernel,
        grid_spec=grid_spec,
        out_shape=jax.ShapeDtypeStruct((MOE_ROWS, D // 2), U32),
        compiler_params=_cp(("arbitrary",)),
        name="moe_down",
    )(route["block_e"], route["n_used"], hidden, w2)


def _route(logits):
    g_logits = logits[:, :N_GROUPS]
    e_logits = logits[:, N_GROUPS:N_GROUPS + N_EXPERTS].reshape(M, N_GROUPS, EPG)
    g_sel = jnp.argmax(g_logits, axis=-1).astype(I32)
    g_w = jnp.take_along_axis(jax.nn.softmax(g_logits, -1), g_sel[:, None], -1)
    e_sel = jnp.take_along_axis(e_logits, g_sel[:, None, None], axis=1)[:, 0]
    top_v, top_i = lax.top_k(e_sel, TOP_K)
    gate = g_w * jax.nn.softmax(top_v, -1)
    expert = g_sel[:, None] * EPG + top_i.astype(I32)
    e_flat = expert.reshape(N_ASSIGN)
    ids = jnp.arange(N_ASSIGN, dtype=I32)
    experts = jnp.arange(N_EXPERTS, dtype=I32)
    e_sorted, order = lax.sort_key_val(e_flat, ids)
    counts = jnp.sum((e_flat[:, None] == experts[None, :]).astype(I32), axis=0)
    padded = (counts + MOE_BM - 1) // MOE_BM * MOE_BM
    start = jnp.cumsum(counts) - counts
    pend = jnp.cumsum(padded)
    shift = pend - padded - start
    dest_sorted = ids + jnp.sum(jnp.where(e_sorted[:, None] == experts[None, :], shift[None, :], 0), axis=1)
    _, dest = lax.sort_key_val(order, dest_sorted)
    block_row = jnp.arange(MOE_BLOCKS, dtype=I32) * MOE_BM
    block_e = jnp.minimum(jnp.sum((pend[None, :] <= block_row[:, None]).astype(I32), axis=1), N_EXPERTS - 1)
    block_off = jnp.sum(jnp.where(block_e[:, None] == experts[None, :], shift[None, :], 0), axis=1)
    return {
        "block_e": block_e.astype(I32),
        "block_off": block_off.astype(I32),
        "n_used": (pend[-1:] // MOE_BM).astype(I32),
        "tok_sorted": jnp.pad(order // TOP_K, (0, MOE_BM)),
        "dest": dest,
        "gate": jnp.pad(gate, ((0, 0), (0, LANES - TOP_K))),
    }


def _rope_tables():
    rows = T_LAT // GRID_W
    row = jnp.repeat(jnp.arange(rows, dtype=I32), GRID_W).astype(F32)
    col = jnp.tile(jnp.arange(GRID_W, dtype=I32), rows).astype(F32)

    def quarter_tables(head_dim):
        m = head_dim // 2
        inv = ROPE_THETA ** (-jnp.arange(0, m, 2, dtype=F32) / m)
        ar = row[:, None] * inv[None, :]
        ac = col[:, None] * inv[None, :]
        cos = jnp.concatenate([jnp.cos(ar), jnp.cos(ar), jnp.cos(ac), jnp.cos(ac)], -1)
        sin = jnp.concatenate([-jnp.sin(ar), jnp.sin(ar), -jnp.sin(ac), jnp.sin(ac)], -1)
        return cos, sin

    c64, s64 = quarter_tables(64)
    c128, s128 = quarter_tables(128)
    return {
        "h64_cos": jnp.concatenate([c64, c64], -1), "h64_sin": jnp.concatenate([s64, s64], -1),
        "h128_cos": c128, "h128_sin": s128,
        "mla_cos": jnp.concatenate([c64, jnp.ones_like(c64)], -1),
        "mla_sin": jnp.concatenate([s64, jnp.zeros_like(s64)], -1),
    }


def _replication_matrix():
    r = np.zeros((SWA_KV * SWA_D, SWA_HEADS * SWA_D), np.float32)
    for g in range(SWA_KV):
        for j in range(SWA_HEADS // SWA_KV):
            for c in range(SWA_D):
                r[g * SWA_D + c, g * 256 + j * SWA_D + c] = 1.0
    return jnp.asarray(r, BF16)


def kernel(x_prompt, x_sample, cache_mla_ckv, cache_mla_kpe, cache_swa_k, cache_swa_v, cache_ax_k, cache_ax_v,
           c, c_ctx, w_in, mla_q_norm, mla_kv_norm, mla_w_q_up, mla_w_kv_up, swa_sink, conv_w, conv_b,
           ax_q_norm, ax_k_norm, w_branch, w_out, w_mod, b_mod, ln1_g, ln1_b, ln2_g, ln2_b,
           moe_w_group, moe_b_group, moe_w_router, moe_b_router, moe_w1, moe_w3, moe_w2):
    x = jnp.concatenate([x_prompt.reshape(M_CTX, D), x_sample.reshape(M_LAT, D)], axis=0)
    cc = jnp.concatenate([c_ctx[None, :], c, jnp.zeros((8 - 1 - N_LAT_SEQ, D), F32)], axis=0)
    mods = _mods(cc, w_mod, b_mod)
    mods3 = [mods[l].reshape(8, 1, 6 * D) for l in range(DEPTH)]

    tabs = _rope_tables()
    rep = _replication_matrix()
    w_in_t = jnp.swapaxes(w_in, 1, 2)
    wq = jnp.pad(mla_w_q_up.reshape(DEPTH, MLA_Q_LORA, MLA_HEADS, MLA_NOPE + MLA_ROPE),
                 ((0, 0), (0, 0), (0, 0), (0, 2 * LANES - MLA_NOPE - MLA_ROPE))
                 ).reshape(DEPTH, MLA_Q_LORA, MLA_HEADS * 2 * LANES).astype(BF16)
    wkv = mla_w_kv_up.astype(BF16)
    cache_kpe = jnp.pad(cache_mla_kpe, ((0, 0), (0, 0), (0, 0), (0, LANES - MLA_ROPE)))
    c_sk = cache_swa_k.reshape(N_LAT_SEQ, DEPTH, PAST, SWA_KV * SWA_D)
    c_sv = cache_swa_v.reshape(N_LAT_SEQ, DEPTH, PAST, SWA_KV * SWA_D)
    c_ak = cache_ax_k.reshape(N_LAT_SEQ, DEPTH, PAST, AX_KV * AX_D)
    c_av = cache_ax_v.reshape(N_LAT_SEQ, DEPTH, PAST, AX_KV * AX_D)
    gq3 = mla_q_norm.reshape(DEPTH, 1, MLA_Q_LORA)
    gkv3 = mla_kv_norm.reshape(DEPTH, 1, MLA_KV_LORA)
    axq3 = ax_q_norm.reshape(DEPTH, 1, AX_D)
    axk3 = ax_k_norm.reshape(DEPTH, 1, AX_D)
    conv_b3 = conv_b.reshape(DEPTH, 1, BRANCH_DIM)
    ln1g, ln1b = ln1_g.reshape(DEPTH, 1, D), ln1_b.reshape(DEPTH, 1, D)
    ln2g, ln2b = ln2_g.reshape(DEPTH, 1, D), ln2_b.reshape(DEPTH, 1, D)
    n_route = N_GROUPS + N_EXPERTS
    w_route = jnp.pad(jnp.concatenate([moe_w_group, moe_w_router], -1), ((0, 0), (0, 0), (0, LANES - n_route)))
    w_route_hi = w_route.astype(BF16)
    w_route = jnp.stack([w_route_hi, (w_route - w_route_hi.astype(F32)).astype(BF16)], axis=1)
    b_route = jnp.pad(jnp.concatenate([moe_b_group, moe_b_router], -1), ((0, 0), (0, LANES - n_route))
                      ).reshape(DEPTH, 1, LANES)

    hb = _modulate(x, mods3[0])
    st = {k: [] for k in ("ckv", "kpe", "sk", "sv", "ak", "av")}
    for l in range(DEPTH):
        y1 = _mm_t(hb, w_in_t, l, 0, 1536, 512, 1024, F32, "w_in_a")
        ykpe = _mm_t(hb, w_in_t, l, 1536, LANES, LANES, 1024, F32, "w_in_kpe")
        y2 = _mm_t(hb, w_in_t, l, 1536, OFF_GATE, W_IN_B_TN, 1024, F32, "w_in_b", shift=True)
        gates = _mm_t(hb, w_in_t, l, 1536 + OFF_GATE, 4 * D, W_IN_B_TN, 1024, BF16, "w_in_gate", shift=True)

        blank = jnp.zeros((M, BRANCH_DIM), BF16)
        oa, ckv_c, kpe_c = _mla(y1, ykpe, None, None, wq, wkv, gq3, gkv3, tabs, l, lat=False, prev=blank)
        (oa,) = _mla(y1, ykpe, cache_mla_ckv, cache_kpe, wq, wkv, gq3, gkv3, tabs, l, lat=True, prev=oa)
        ob = _swa(y2, None, None, swa_sink[l], rep, tabs, l, lat=False, prev=blank)
        ob = _swa(y2, c_sk, c_sv, swa_sink[l], rep, tabs, l, lat=True, prev=ob)
        oc = _conv(y2, conv_w, conv_b3, l)
        od, ak_c = _ax(y2, None, None, axq3, axk3, tabs, l, lat=False, prev=blank)
        (od,) = _ax(y2, c_ak, c_av, axq3, axk3, tabs, l, lat=True, prev=od)

        st["ckv"].append(ckv_c)
        st["kpe"].append(kpe_c)
        st["sk"].append(y2[:M_CTX, OFF_SK:OFF_SK + 256].reshape(N_CTX_SEQ, T_CTX, SWA_KV, SWA_D))
        st["sv"].append(y2[:M_CTX, OFF_SV:OFF_SV + 256].reshape(N_CTX_SEQ, T_CTX, SWA_KV, SWA_D))
        st["ak"].append(ak_c.reshape(N_CTX_SEQ, T_CTX, AX_KV, AX_D))
        st["av"].append(y2[:M_CTX, OFF_AV:OFF_AV + 512].reshape(N_CTX_SEQ, T_CTX, AX_KV, AX_D))

        merged = _merge([oa, ob, oc, od], w_branch, gates, l)
        mix = _mm(merged, w_out, l, D, 512, 1024, BF16, "w_out")
        x1, h2, logits = _ln1(x, mix, mods3[l], ln1g, ln1b, w_route, b_route, l)

        route = _route(logits)
        hidden = _moe_up(route, h2, moe_w1, moe_w3, l)
        y_rows = _moe_down(route, hidden, moe_w2, l)
        x, hb = _ln2(x1, y_rows, route["dest"], route["gate"], mods3[l], mods3[min(l + 1, DEPTH - 1)],
                     ln2g, ln2b, l)

    y_p = x[:M_CTX].reshape(N_CTX_SEQ, T_CTX, D)
    y_s = x[M_CTX:].reshape(N_LAT_SEQ, T_LAT, D)
    return (y_p, y_s,
            jnp.stack(st["ckv"], axis=1), jnp.stack(st["kpe"], axis=1),
            jnp.stack(st["sk"], axis=1), jnp.stack(st["sv"], axis=1),
            jnp.stack(st["ak"], axis=1), jnp.stack(st["av"], axis=1))
```
